```python
import jax, jax.numpy as jnp
from jax import lax
import numpy as np

D_MODEL = 1024
BATCH = 16
SEQ = 256
DEPTH = 2
DEC_BATCH = 2
DEC_SEQ = 1024
PAST_LEN = 256

GRID_W = 64
N_RET_HEADS = 4
RET_DK = 128
RET_DV = 256
RET_CHUNK = 128
N_GMLP_GROUPS = 4
GMLP_CH = 128
GMLP_CHUNK = 128
RET_QK_W = N_RET_HEADS * RET_DK
RET_V_W = N_RET_HEADS * RET_DV
GMLP_W = N_GMLP_GROUPS * GMLP_CH
EVEN_IN_W = 2 * RET_QK_W + 2 * RET_V_W + 2 * GMLP_W
EVEN_OUT_W = RET_V_W + GMLP_W
CONV_CH = D_MODEL
CONV_K = 3
D_FF = 4 * D_MODEL
N_MOD = 6
EPS = 1e-6

kernel_name = "hybrid_retention_gmlp_shortconv_diffusion_step"


def rms_norm(x, g):
    xf = x.astype(jnp.float32)
    y = xf * lax.rsqrt(jnp.mean(xf * xf, axis=-1, keepdims=True) + EPS)
    return (y * g.astype(jnp.float32)).astype(x.dtype)


def adaln(cond, w_mod, b_mod):
    m = (jax.nn.silu(cond) @ w_mod + b_mod)[:, None, :]
    return jnp.split(m, N_MOD, axis=-1)


def _retention_dir(q, k, v, log_gamma, s0):
    b, L, h, _ = q.shape
    dv = v.shape[-1]
    n = L // RET_CHUNK

    def to_chunks(t):
        return t.reshape(b, n, RET_CHUNK, h, t.shape[-1]).transpose(1, 0, 3, 2, 4)

    qc, kc, vc = to_chunks(q), to_chunks(k), to_chunks(v)
    pos = jnp.arange(RET_CHUNK, dtype=jnp.float32)
    diff = pos[:, None] - pos[None, :]
    decay_mask = jnp.where(diff >= 0, jnp.exp(jnp.maximum(diff, 0.0) * log_gamma[:, None, None]), 0.0)
    q_decay = jnp.exp((pos + 1.0) * log_gamma[:, None])[:, :, None]
    k_decay = jnp.exp((RET_CHUNK - 1.0 - pos) * log_gamma[:, None])[:, :, None]
    chunk_decay = jnp.exp(RET_CHUNK * log_gamma)[:, None, None]

    def step(s, inp):
        qi, ki, vi = inp
        scores = jnp.einsum("bhid,bhjd->bhij", qi, ki) * decay_mask
        inner = jnp.einsum("bhij,bhje->bhie", scores, vi)
        cross = jnp.einsum("bhid,bhde->bhie", qi, s) * q_decay
        s_new = chunk_decay * s + jnp.einsum("bhjd,bhje->bhde", ki * k_decay, vi)
        return s_new, inner + cross

    s_fin, out = lax.scan(step, s0, (qc, kc, vc))
    out = out.transpose(1, 0, 3, 2, 4).reshape(b, L, h, dv)
    return out, s_fin


def bidir_retention(q, k, v, log_gamma, s0):
    o_f, s_f = _retention_dir(q, k, v, log_gamma[0], s0[:, 0])
    o_b, s_b = _retention_dir(jnp.flip(q, 1), jnp.flip(k, 1), jnp.flip(v, 1), log_gamma[1], s0[:, 1])
    return o_f + jnp.flip(o_b, 1), jnp.stack([s_f, s_b], axis=1)


def even_mixer(h, s0, w_in, ret_decay_exp, gmlp_ws, gmlp_b, w_out):
    b, L, _ = h.shape
    proj = h @ w_in
    q, k, v, g, u, z = jnp.split(
        proj,
        [RET_QK_W, 2 * RET_QK_W, 2 * RET_QK_W + RET_V_W, 2 * RET_QK_W + 2 * RET_V_W,
         2 * RET_QK_W + 2 * RET_V_W + GMLP_W],
        axis=-1)
    qf = q.reshape(b, L, N_RET_HEADS, RET_DK).astype(jnp.float32)
    kf = k.reshape(b, L, N_RET_HEADS, RET_DK).astype(jnp.float32) * (RET_DK ** -0.5)
    vf = v.reshape(b, L, N_RET_HEADS, RET_DV).astype(jnp.float32)
    log_gamma = jnp.log1p(-jnp.exp2(-ret_decay_exp.astype(jnp.float32)))
    o, s_fin = bidir_retention(qf, kf, vf, log_gamma, s0.astype(jnp.float32))
    o = o * lax.rsqrt(jnp.mean(o * o, axis=-1, keepdims=True) + EPS)
    ret_out = (jax.nn.silu(g.astype(jnp.float32)) * o.reshape(b, L, RET_V_W)).astype(h.dtype)
    u = jax.nn.gelu(u)
    zf = jax.nn.gelu(z).astype(jnp.float32)
    mu = jnp.mean(zf, axis=-1, keepdims=True)
    zf = (zf - mu) * lax.rsqrt(jnp.mean((zf - mu) ** 2, axis=-1, keepdims=True) + EPS)
    zc = zf.astype(h.dtype).reshape(b, L // GMLP_CHUNK, GMLP_CHUNK, N_GMLP_GROUPS, GMLP_CH)
    sv = jnp.einsum("gpq,bnqgc->bnpgc", gmlp_ws, zc) + gmlp_b.T[None, None, :, :, None]
    gm_out = u * sv.reshape(b, L, GMLP_W)
    out = jnp.concatenate([ret_out, gm_out], axis=-1) @ w_out
    return out, s_fin.astype(h.dtype)


def conv3(x, w):
    pad = [(0, 0)] * (x.ndim - 2) + [(1, 1), (0, 0)]
    xp = jnp.pad(x, pad)
    return w[0] * xp[..., :-2, :] + w[1] * xp[..., 1:-1, :] + w[2] * xp[..., 2:, :]


def odd_mixer(h, w_in, conv_w, w_out, rows):
    b, L, _ = h.shape
    bg, cg, hv = jnp.split(h @ w_in, 3, axis=-1)
    xc = cg * hv
    if rows is None:
        yc = conv3(xc, conv_w)
    else:
        yc = conv3(xc.reshape(b, rows, GRID_W, CONV_CH), conv_w).reshape(b, L, CONV_CH)
    return (bg * yc) @ w_out


def channel_mlp(h, w1, w2):
    a = jax.nn.relu(h @ w1)
    return (a * a) @ w2


def trunk(x, cond, ret_states, layers, final_norm, rows):
    new_states = []
    for i in range(DEPTH):
        p = layers[i]
        sh1, sc1, g1, sh2, sc2, g2 = adaln(cond, p["w_mod"], p["b_mod"])
        hmod = rms_norm(x, p["norm1"]) * (1.0 + sc1) + sh1
        if i % 2 == 0:
            mix, s_new = even_mixer(hmod, ret_states[i // 2], p["w_in"], p["ret_decay_exp"],
                                    p["gmlp_ws"], p["gmlp_b"], p["w_out"])
            new_states.append(s_new)
        else:
            mix = odd_mixer(hmod, p["w_in"], p["conv_w"], p["w_out"], rows)
        x = x + g1 * mix
        hmod = rms_norm(x, p["norm2"]) * (1.0 + sc2) + sh2
        x = x + g2 * channel_mlp(hmod, p["ffn_w1"], p["ffn_w2"])
    return rms_norm(x, final_norm), new_states


def setup_inputs(seed: int = 0) -> dict:
    key = jax.random.key(seed)
    ks = jax.random.split(key, 32)
    nrm = lambda k, shape, s: jax.random.normal(k, shape, jnp.float32) * s
    d = D_MODEL
    inp = {}
    inp["x_prompt"] = nrm(ks[0], (BATCH, SEQ, d), 1.0)
    inp["x_sample"] = nrm(ks[1], (DEC_BATCH, DEC_SEQ, d), 1.0)
    inp["state_l0_ret"] = nrm(ks[2], (DEC_BATCH, 2, N_RET_HEADS, RET_DK, RET_DV), 0.5)
    inp["c"] = nrm(ks[3], (DEC_BATCH, d), 1.0)
    inp["c_ctx"] = nrm(ks[4], (d,), 1.0)
    inp["l0_norm1"] = 1.0 + nrm(ks[5], (d,), 0.1)
    inp["l0_w_in"] = nrm(ks[6], (d, EVEN_IN_W), d ** -0.5)
    inp["l0_ret_decay_exp"] = (5.0 + jnp.arange(N_RET_HEADS, dtype=jnp.float32))[None, :] + nrm(ks[7], (2, N_RET_HEADS), 0.1)
    inp["l0_gmlp_ws"] = nrm(ks[8], (N_GMLP_GROUPS, GMLP_CHUNK, GMLP_CHUNK), GMLP_CHUNK ** -0.5)
    inp["l0_gmlp_b"] = 1.0 + nrm(ks[9], (N_GMLP_GROUPS, GMLP_CHUNK), 0.1)
    inp["l0_w_out"] = nrm(ks[10], (EVEN_OUT_W, d), EVEN_OUT_W ** -0.5)
    inp["l0_norm2"] = 1.0 + nrm(ks[11], (d,), 0.1)
    inp["l0_w_mod"] = nrm(ks[12], (d, N_MOD * d), 0.5 * d ** -0.5)
    inp["l0_b_mod"] = nrm(ks[13], (N_MOD * d,), 0.02)
    inp["l0_ffn_w1"] = nrm(ks[14], (d, D_FF), d ** -0.5)
    inp["l0_ffn_w2"] = nrm(ks[15], (D_FF, d), D_FF ** -0.5)
    inp["l1_norm1"] = 1.0 + nrm(ks[16], (d,), 0.1)
    inp["l1_w_in"] = nrm(ks[17], (d, 3 * CONV_CH), d ** -0.5)
    inp["l1_conv_w"] = nrm(ks[18], (CONV_K, CONV_CH), CONV_K ** -0.5)
    inp["l1_w_out"] = nrm(ks[19], (CONV_CH, d), CONV_CH ** -0.5)
    inp["l1_norm2"] = 1.0 + nrm(ks[20], (d,), 0.1)
    inp["l1_w_mod"] = nrm(ks[21], (d, N_MOD * d), 0.5 * d ** -0.5)
    inp["l1_b_mod"] = nrm(ks[22], (N_MOD * d,), 0.02)
    inp["l1_ffn_w1"] = nrm(ks[23], (d, D_FF), d ** -0.5)
    inp["l1_ffn_w2"] = nrm(ks[24], (D_FF, d), D_FF ** -0.5)
    inp["final_norm"] = 1.0 + nrm(ks[25], (d,), 0.1)
    return inp


def reference(x_prompt, x_sample, state_l0_ret, c, c_ctx,
              l0_norm1, l0_w_in, l0_ret_decay_exp, l0_gmlp_ws, l0_gmlp_b, l0_w_out, l0_norm2,
              l0_w_mod, l0_b_mod, l0_ffn_w1, l0_ffn_w2,
              l1_norm1, l1_w_in, l1_conv_w, l1_w_out, l1_norm2, l1_w_mod, l1_b_mod, l1_ffn_w1, l1_ffn_w2,
              final_norm):
    layers = [
        {"norm1": l0_norm1, "w_in": l0_w_in, "ret_decay_exp": l0_ret_decay_exp, "gmlp_ws": l0_gmlp_ws,
         "gmlp_b": l0_gmlp_b, "w_out": l0_w_out, "norm2": l0_norm2, "w_mod": l0_w_mod, "b_mod": l0_b_mod,
         "ffn_w1": l0_ffn_w1, "ffn_w2": l0_ffn_w2},
        {"norm1": l1_norm1, "w_in": l1_w_in, "conv_w": l1_conv_w, "w_out": l1_w_out, "norm2": l1_norm2,
         "w_mod": l1_w_mod, "b_mod": l1_b_mod, "ffn_w1": l1_ffn_w1, "ffn_w2": l1_ffn_w2},
    ]
    zero_state = jnp.zeros((x_prompt.shape[0], 2, N_RET_HEADS, RET_DK, RET_DV), x_prompt.dtype)
    y_prompt, ctx_states = trunk(x_prompt, c_ctx[None, :], [zero_state], layers, final_norm, None)
    new_state_l0_ret = ctx_states[0]
    rows = x_sample.shape[1] // GRID_W
    y_sample, _ = trunk(x_sample, c, [state_l0_ret], layers, final_norm, rows)
    return (y_prompt, y_sample, new_state_l0_ret)
```

```python
import functools

import jax
import jax.numpy as jnp
from jax import lax
from jax.experimental import pallas as pl
from jax.experimental.pallas import tpu as pltpu

F32 = jnp.float32
BF16 = jnp.bfloat16

D_MODEL = 1024
N_CTX_SEQ = 16
CTX_LEN = 256
N_LAT_SEQ = 2
LAT_LEN = 1024
GRID_W = 64
N_CTX_TOK = N_CTX_SEQ * CTX_LEN
N_TOK = N_CTX_TOK + N_LAT_SEQ * LAT_LEN
N_HEADS = 4
DK = 128
DV = 256
CHUNK = 128
N_GROUPS = 4
GMLP_CH = 128
QK_W = N_HEADS * DK
V_W = N_HEADS * DV
GMLP_W = N_GROUPS * GMLP_CH
IN0_W = 2 * QK_W + 2 * V_W + 2 * GMLP_W
D_FF = 4 * D_MODEL
N_MOD = 6
EPS = 1e-6
N_COND = 3
COND_ROWS = 16

N_CTX_CHUNKS = N_CTX_TOK // CHUNK
N_CHUNKS = N_TOK // CHUNK
CTX_CHUNKS_PER_SEQ = CTX_LEN // CHUNK
LAT_CHUNKS_PER_SEQ = LAT_LEN // CHUNK

TOKEN_TILE = 256
MOD_COL_TILE = 1536
VMEM_LIMIT_BYTES = 56 * 1024 * 1024


def _params(n_axes, semantics="arbitrary"):
    return pltpu.CompilerParams(dimension_semantics=(semantics,) * n_axes,
                                vmem_limit_bytes=VMEM_LIMIT_BYTES)


def _resident(shape):
    return pl.BlockSpec(shape, lambda *_: (0,) * len(shape), pipeline_mode=pl.Buffered(1))


def _tile_cond(i, tile):
    n_ctx_tiles = N_CTX_TOK // tile
    return jnp.where(i < n_ctx_tiles, 0, 1 + (i - n_ctx_tiles) // (LAT_LEN // tile))


def _rms_mod(x, norm_w, shift, scale):
    y = x * lax.rsqrt(jnp.mean(x * x, axis=-1, keepdims=True) + EPS)
    return (y * norm_w) * (1.0 + scale) + shift


def _dot(a, b):
    return jnp.dot(a.astype(BF16), b.astype(BF16), preferred_element_type=F32)


def _mod_kernel(cond_ref, w_ref, b_ref, o_ref):
    cond = cond_ref[...]
    o_ref[...] = _dot(jax.nn.silu(cond), w_ref[...]) + b_ref[...]


def _mod_table(cond, w_mod, b_mod):
    n_out = N_MOD * D_MODEL
    out = pl.pallas_call(
        _mod_kernel,
        grid=(n_out // MOD_COL_TILE,),
        in_specs=[
            pl.BlockSpec((COND_ROWS, D_MODEL), lambda j: (0, 0)),
            pl.BlockSpec((D_MODEL, MOD_COL_TILE), lambda j: (0, j)),
            pl.BlockSpec((1, MOD_COL_TILE), lambda j: (0, j)),
        ],
        out_specs=pl.BlockSpec((COND_ROWS, MOD_COL_TILE), lambda j: (0, j)),
        out_shape=jax.ShapeDtypeStruct((COND_ROWS, n_out), F32),
        compiler_params=_params(1),
        name="adaln_table",
    )(cond, w_mod, b_mod.reshape(1, n_out))
    return out[:N_COND].reshape(N_COND, N_MOD, D_MODEL)


def _in0_kernel(x_ref, mod_ref, nw_ref, w_ref, o_ref):
    h = _rms_mod(x_ref[...], nw_ref[...], mod_ref[0, 0:1, :], mod_ref[0, 1:2, :]).astype(BF16)

    def proj(lo, hi):
        return jnp.dot(h, w_ref[:, lo:hi], preferred_element_type=F32)

    q_lo, k_lo, v_lo, g_lo, u_lo, z_lo = 0, QK_W, 2 * QK_W, 2 * QK_W + V_W, 2 * QK_W + 2 * V_W, IN0_W - GMLP_W
    o_ref[:, q_lo:k_lo] = proj(q_lo, k_lo).astype(BF16)
    o_ref[:, k_lo:v_lo] = (proj(k_lo, v_lo) * (DK ** -0.5)).astype(BF16)
    o_ref[:, v_lo:g_lo] = proj(v_lo, g_lo).astype(BF16)
    o_ref[:, g_lo:u_lo] = jax.nn.silu(proj(g_lo, u_lo)).astype(BF16)
    o_ref[:, u_lo:z_lo] = jax.nn.gelu(proj(u_lo, z_lo)).astype(BF16)
    z = jax.nn.gelu(proj(z_lo, IN0_W))
    zc = z - jnp.mean(z, axis=-1, keepdims=True)
    o_ref[:, z_lo:IN0_W] = (zc * lax.rsqrt(jnp.mean(zc * zc, axis=-1, keepdims=True) + EPS)).astype(BF16)


def _in0(x, mod, norm_w, w_in):
    tile = TOKEN_TILE
    return pl.pallas_call(
        _in0_kernel,
        grid=(N_TOK // tile,),
        in_specs=[
            pl.BlockSpec((tile, D_MODEL), lambda i: (i, 0)),
            pl.BlockSpec((1, N_MOD, D_MODEL), lambda i: (_tile_cond(i, tile), 0, 0)),
            _resident((1, D_MODEL)),
            _resident((D_MODEL, IN0_W)),
        ],
        out_specs=pl.BlockSpec((tile, IN0_W), lambda i: (i, 0)),
        out_shape=jax.ShapeDtypeStruct((N_TOK, IN0_W), BF16),
        compiler_params=_params(1),
        name="l0_in_proj",
    )(x, mod, norm_w.reshape(1, D_MODEL), w_in)


def _scan_chunk(d, c):
    return jnp.where(d == 0, c, N_CHUNKS - 1 - c)


def _chunk_lat_seq(cc):
    return jnp.clip((cc - N_CTX_CHUNKS) // LAT_CHUNKS_PER_SEQ, 0, N_LAT_SEQ - 1)


def _chunk_ctx_seq(cc):
    return jnp.minimum(cc // CTX_CHUNKS_PER_SEQ, N_CTX_SEQ - 1)


def _ret_kernel(lg_ref, q_ref, k_ref, v_ref, s0_ref, o_ref, sfin_ref,
                state, mask, q_decay, k_decay, chunk_decay):
    d = pl.program_id(0)
    c = pl.program_id(1)
    cc = _scan_chunk(d, c)
    is_ctx = cc < N_CTX_CHUNKS
    per_seq = jnp.where(is_ctx, CTX_CHUNKS_PER_SEQ, LAT_CHUNKS_PER_SEQ)
    in_seq = jnp.where(is_ctx, cc, cc - N_CTX_CHUNKS) % per_seq
    first, last = in_seq == 0, in_seq == per_seq - 1
    is_start = jnp.where(d == 0, first, last)
    is_end = jnp.where(d == 0, last, first)

    @pl.when(c == 0)
    def _build_decays():
        row = lax.broadcasted_iota(jnp.int32, (CHUNK, CHUNK), 0)
        col = lax.broadcasted_iota(jnp.int32, (CHUNK, CHUNK), 1)
        diff = jnp.where(d == 0, row - col, col - row).astype(F32)
        pos = lax.broadcasted_iota(jnp.int32, (CHUNK, 1), 0)
        pos = jnp.where(d == 0, pos, CHUNK - 1 - pos).astype(F32)
        for h in range(N_HEADS):
            lg = lg_ref[d, h]
            mask[h] = jnp.where(diff >= 0, jnp.exp(jnp.maximum(diff, 0.0) * lg), 0.0)
            q_decay[h] = jnp.exp((pos + 1.0) * lg)
            k_decay[h] = jnp.exp((CHUNK - 1.0 - pos) * lg)
            chunk_decay[h] = jnp.exp(jnp.full((1, DV), CHUNK, F32) * lg)

    @pl.when(is_start & is_ctx)
    def _zero_state():
        state[...] = jnp.zeros_like(state)

    @pl.when(is_start & jnp.logical_not(is_ctx))
    def _cached_state():
        state[...] = s0_ref[0, 0]

    for h in range(N_HEADS):
        qh = q_ref[:, h * DK:(h + 1) * DK]
        kh = k_ref[:, h * DK:(h + 1) * DK]
        vh = v_ref[:, h * DV:(h + 1) * DV]
        s = state[h]
        scores = lax.dot_general(qh, kh, (((1,), (1,)), ((), ())), preferred_element_type=F32) * mask[h]
        inner = jnp.dot(scores.astype(BF16), vh, preferred_element_type=F32)
        cross = jnp.dot(qh, s.astype(BF16), preferred_element_type=F32) * q_decay[h]
        k_dec = (kh.astype(F32) * k_decay[h]).astype(BF16)
        update = lax.dot_general(k_dec, vh, (((0,), (0,)), ((), ())), preferred_element_type=F32)
        o_ref[0, :, h * DV:(h + 1) * DV] = (inner + cross).astype(o_ref.dtype)
        state[h] = chunk_decay[h] * s + update

    @pl.when(is_end & is_ctx)
    def _emit_final_state():
        sfin_ref[0, 0] = state[...]


def _retention(log_gamma, proj, s0):
    q_spec = pl.BlockSpec((CHUNK, QK_W), lambda d, c: (_scan_chunk(d, c), 0))
    k_spec = pl.BlockSpec((CHUNK, QK_W), lambda d, c: (_scan_chunk(d, c), 1))
    v_spec = pl.BlockSpec((CHUNK, V_W), lambda d, c: (_scan_chunk(d, c), 1))
    s0_spec = pl.BlockSpec((1, 1, N_HEADS, DK, DV),
                           lambda d, c: (_chunk_lat_seq(_scan_chunk(d, c)), d, 0, 0, 0))
    o_spec = pl.BlockSpec((1, CHUNK, V_W), lambda d, c: (d, _scan_chunk(d, c), 0))
    sfin_spec = pl.BlockSpec((1, 1, N_HEADS, DK, DV),
                             lambda d, c: (_chunk_ctx_seq(_scan_chunk(d, c)), d, 0, 0, 0))
    return pl.pallas_call(
        _ret_kernel,
        grid=(2, N_CHUNKS),
        in_specs=[pl.BlockSpec(memory_space=pltpu.SMEM), q_spec, k_spec, v_spec, s0_spec],
        out_specs=[o_spec, sfin_spec],
        out_shape=[jax.ShapeDtypeStruct((2, N_TOK, V_W), BF16),
                   jax.ShapeDtypeStruct((N_CTX_SEQ, 2, N_HEADS, DK, DV), F32)],
        scratch_shapes=[
            pltpu.VMEM((N_HEADS, DK, DV), F32),
            pltpu.VMEM((N_HEADS, CHUNK, CHUNK), F32),
            pltpu.VMEM((N_HEADS, CHUNK, 1), F32),
            pltpu.VMEM((N_HEADS, CHUNK, 1), F32),
            pltpu.VMEM((N_HEADS, 1, DV), F32),
        ],
        compiler_params=_params(2),
        name="l0_retention",
    )(log_gamma, proj, proj, proj, s0)


def _mlp_residual(x, mod_ref, nw_ref, w1_ref, w2_ref):
    h = _rms_mod(x, nw_ref[...], mod_ref[0, 3:4, :], mod_ref[0, 4:5, :])
    a = jnp.maximum(_dot(h, w1_ref[...]), 0.0)
    return x + mod_ref[0, 5:6, :] * _dot(a * a, w2_ref[...])


def _out0_kernel(x_ref, o_ref, g_ref, uz_ref, mod_ref, nw_ref, ws_ref, gb_ref,
                 wout_ref, w1_ref, w2_ref, y_ref):
    tile = x_ref.shape[0]
    o = o_ref[0].astype(F32) + o_ref[1].astype(F32)
    heads = []
    for h in range(N_HEADS):
        oh = o[:, h * DV:(h + 1) * DV]
        heads.append(oh * lax.rsqrt(jnp.mean(oh * oh, axis=-1, keepdims=True) + EPS))
    ret_out = g_ref[...].astype(F32) * jnp.concatenate(heads, axis=-1)

    rows = []
    for n in range(tile // CHUNK):
        groups = []
        for g in range(N_GROUPS):
            zc = uz_ref[n * CHUNK:(n + 1) * CHUNK, GMLP_W + g * GMLP_CH:GMLP_W + (g + 1) * GMLP_CH]
            uc = uz_ref[n * CHUNK:(n + 1) * CHUNK, g * GMLP_CH:(g + 1) * GMLP_CH]
            sv = jnp.dot(ws_ref[g], zc, preferred_element_type=F32) + gb_ref[:, g:g + 1]
            groups.append(uc.astype(F32) * sv)
        rows.append(jnp.concatenate(groups, axis=-1))
    gm_out = jnp.concatenate(rows, axis=0)

    mix = _dot(ret_out, wout_ref[:V_W, :]) + _dot(gm_out, wout_ref[V_W:, :])
    x1 = x_ref[...] + mod_ref[0, 2:3, :] * mix
    y_ref[...] = _mlp_residual(x1, mod_ref, nw_ref, w1_ref, w2_ref)


def _out0(x, o, proj, mod, norm_w, gmlp_ws, gmlp_b_t, w_out, w1, w2):
    tile = TOKEN_TILE
    return pl.pallas_call(
        _out0_kernel,
        grid=(N_TOK // tile,),
        in_specs=[
            pl.BlockSpec((tile, D_MODEL), lambda i: (i, 0)),
            pl.BlockSpec((2, tile, V_W), lambda i: (0, i, 0)),
            pl.BlockSpec((tile, V_W), lambda i: (i, 2)),
            pl.BlockSpec((tile, 2 * GMLP_W), lambda i: (i, 3)),
            pl.BlockSpec((1, N_MOD, D_MODEL), lambda i: (_tile_cond(i, tile), 0, 0)),
            _resident((1, D_MODEL)),
            _resident((N_GROUPS, CHUNK, CHUNK)),
            _resident((CHUNK, N_GROUPS)),
            _resident((V_W + GMLP_W, D_MODEL)),
            _resident((D_MODEL, D_FF)),
            _resident((D_FF, D_MODEL)),
        ],
        out_specs=pl.BlockSpec((tile, D_MODEL), lambda i: (i, 0)),
        out_shape=jax.ShapeDtypeStruct((N_TOK, D_MODEL), F32),
        compiler_params=_params(1),
        name="l0_out_mlp",
    )(x, o, proj, proj, mod, norm_w.reshape(1, D_MODEL), gmlp_ws, gmlp_b_t, w_out, w1, w2)


def _l1_kernel(x_ref, mod_ref, n1_ref, win_ref, cw_ref, wout_ref, n2_ref, w1_ref, w2_ref, fn_ref, y_ref):
    tile = x_ref.shape[0]
    i = pl.program_id(0)
    x = x_ref[...]
    h = _rms_mod(x, n1_ref[...], mod_ref[0, 0:1, :], mod_ref[0, 1:2, :]).astype(BF16)
    bg = jnp.dot(h, win_ref[:, 0:D_MODEL], preferred_element_type=F32)
    cg = jnp.dot(h, win_ref[:, D_MODEL:2 * D_MODEL], preferred_element_type=F32)
    hv = jnp.dot(h, win_ref[:, 2 * D_MODEL:3 * D_MODEL], preferred_element_type=F32)
    xc = cg * hv
    period = jnp.where(i < N_CTX_TOK // tile, CTX_LEN, GRID_W)
    t = lax.broadcasted_iota(jnp.int32, (tile, 1), 0) & (period - 1)
    prev = jnp.where(t == 0, 0.0, pltpu.roll(xc, 1, 0))
    nxt = jnp.where(t == period - 1, 0.0, pltpu.roll(xc, tile - 1, 0))
    yc = cw_ref[0:1, :] * prev + cw_ref[1:2, :] * xc + cw_ref[2:3, :] * nxt
    x1 = x + mod_ref[0, 2:3, :] * _dot(bg * yc, wout_ref[...])
    x2 = _mlp_residual(x1, mod_ref, n2_ref, w1_ref, w2_ref)
    y_ref[...] = (x2 * lax.rsqrt(jnp.mean(x2 * x2, axis=-1, keepdims=True) + EPS)) * fn_ref[...]


def _l1(x, mod, norm1, w_in, conv_w, w_out, norm2, w1, w2, final_norm):
    tile = TOKEN_TILE
    return pl.pallas_call(
        _l1_kernel,
        grid=(N_TOK // tile,),
        in_specs=[
            pl.BlockSpec((tile, D_MODEL), lambda i: (i, 0)),
            pl.BlockSpec((1, N_MOD, D_MODEL), lambda i: (_tile_cond(i, tile), 0, 0)),
            _resident((1, D_MODEL)),
            _resident((D_MODEL, 3 * D_MODEL)),
            _resident((3, D_MODEL)),
            _resident((D_MODEL, D_MODEL)),
            _resident((1, D_MODEL)),
            _resident((D_MODEL, D_FF)),
            _resident((D_FF, D_MODEL)),
            _resident((1, D_MODEL)),
        ],
        out_specs=pl.BlockSpec((tile, D_MODEL), lambda i: (i, 0)),
        out_shape=jax.ShapeDtypeStruct((N_TOK, D_MODEL), F32),
        compiler_params=_params(1),
        name="l1_conv_mlp_norm",
    )(x, mod, norm1.reshape(1, D_MODEL), w_in, conv_w, w_out, norm2.reshape(1, D_MODEL), w1, w2,
      final_norm.reshape(1, D_MODEL))


def kernel(x_prompt, x_sample, state_l0_ret, c, c_ctx, l0_norm1, l0_w_in, l0_ret_decay_exp, l0_gmlp_ws,
           l0_gmlp_b, l0_w_out, l0_norm2, l0_w_mod, l0_b_mod, l0_ffn_w1, l0_ffn_w2, l1_norm1, l1_w_in,
           l1_conv_w, l1_w_out, l1_norm2, l1_w_mod, l1_b_mod, l1_ffn_w1, l1_ffn_w2, final_norm):
    x = jnp.concatenate([x_prompt.reshape(N_CTX_TOK, D_MODEL), x_sample.reshape(-1, D_MODEL)], axis=0)
    cond = jnp.concatenate([c_ctx[None, :], c, jnp.zeros((COND_ROWS - N_COND, D_MODEL), F32)], axis=0)
    log_gamma = jnp.log1p(-jnp.exp2(-l0_ret_decay_exp.astype(F32)))

    mod0 = _mod_table(cond, l0_w_mod, l0_b_mod)
    mod1 = _mod_table(cond, l1_w_mod, l1_b_mod)

    proj = _in0(x, mod0, l0_norm1, l0_w_in.astype(BF16))
    o, new_state = _retention(log_gamma, proj, state_l0_ret)
    x = _out0(x, o, proj, mod0, l0_norm2, l0_gmlp_ws.astype(BF16), l0_gmlp_b.T,
              l0_w_out.astype(BF16), l0_ffn_w1.astype(BF16), l0_ffn_w2.astype(BF16))
    y = _l1(x, mod1, l1_norm1, l1_w_in.astype(BF16), l1_conv_w, l1_w_out.astype(BF16), l1_norm2,
            l1_ffn_w1.astype(BF16), l1_ffn_w2.astype(BF16), final_norm)

    y_prompt = y[:N_CTX_TOK].reshape(N_CTX_SEQ, CTX_LEN, D_MODEL)
    y_sample = y[N_CTX_TOK:].reshape(N_LAT_SEQ, LAT_LEN, D_MODEL)
    return (y_prompt, y_sample, new_state)
```

```python
import jax
import jax.numpy as jnp
from jax import lax
from jax.experimental import pallas as pl
from jax.experimental.pallas import tpu as pltpu

F32 = jnp.float32
BF16 = jnp.bfloat16

D_MODEL = 1024
N_CTX_SEQ = 16
CTX_LEN = 256
N_LAT_SEQ = 2
LAT_LEN = 1024
GRID_W = 64
N_CTX_TOK = N_CTX_SEQ * CTX_LEN
N_LAT_TOK = N_LAT_SEQ * LAT_LEN
N_TOK = N_CTX_TOK + N_LAT_TOK
N_HEADS = 4
DK = 128
DV = 256
CHUNK = 128
N_GROUPS = 4
GMLP_CH = 128
QK_W = N_HEADS * DK
V_W = N_HEADS * DV
GMLP_W = N_GROUPS * GMLP_CH
IN0_W = 2 * QK_W + 2 * V_W + 2 * GMLP_W
OUT0_W = V_W + GMLP_W
D_FF = 4 * D_MODEL
N_MOD = 6
EPS = 1e-6
N_COND = 3
COND_ROWS = 16

TILE = 256
N_TILES = N_TOK // TILE
N_CTX_TILES = N_CTX_TOK // TILE
N_LAT_TILES = N_LAT_TOK // TILE
LAT_TILES_PER_SEQ = LAT_LEN // TILE
N_CAST_SLABS = 16
MOD_COL_TILE = 1536
VMEM_LIMIT_BYTES = 56 * 1024 * 1024


def _params(n_axes):
    return pltpu.CompilerParams(dimension_semantics=("arbitrary",) * n_axes,
                                vmem_limit_bytes=VMEM_LIMIT_BYTES)


def _resident(shape):
    return pl.BlockSpec(shape, lambda *_: (0,) * len(shape), pipeline_mode=pl.Buffered(1))


def _tile_cond(tile):
    return jnp.where(tile < N_CTX_TILES, 0, 1 + (tile - N_CTX_TILES) // LAT_TILES_PER_SEQ)


def _rms_mod(x, norm_w, shift, scale):
    y = x * lax.rsqrt(jnp.mean(x * x, axis=-1, keepdims=True) + EPS)
    return (y * norm_w) * (1.0 + scale) + shift


def _dot(a, b):
    return jnp.dot(a.astype(BF16), b.astype(BF16), preferred_element_type=F32)


def _dot_nt(a, b):
    return lax.dot_general(a, b, (((1,), (1,)), ((), ())), preferred_element_type=F32)


def _dot_tn(a, b):
    return lax.dot_general(a, b, (((0,), (0,)), ((), ())), preferred_element_type=F32)


def _cast_in_specs(weights):
    return [pl.BlockSpec((w.shape[0] // N_CAST_SLABS, w.shape[1]),
                         lambda s: (jnp.minimum(s, N_CAST_SLABS - 1), 0)) for w in weights]


def _cast_out_shapes(weights):
    return [jax.ShapeDtypeStruct(w.shape, BF16) for w in weights]


def _cast_slabs(step, in_refs, out_refs):
    @pl.when(step < N_CAST_SLABS)
    def _():
        for src, dst in zip(in_refs, out_refs):
            dst[...] = src[...].astype(BF16)


def _mod_kernel(cond_ref, w_ref, b_ref, o_ref):
    o_ref[...] = _dot(jax.nn.silu(cond_ref[...]), w_ref[...]) + b_ref[...]


def _mod_table(cond, w_mod, b_mod):
    n_out = N_MOD * D_MODEL
    out = pl.pallas_call(
        _mod_kernel,
        grid=(n_out // MOD_COL_TILE,),
        in_specs=[
            pl.BlockSpec((COND_ROWS, D_MODEL), lambda j: (0, 0)),
            pl.BlockSpec((D_MODEL, MOD_COL_TILE), lambda j: (0, j)),
            pl.BlockSpec((1, MOD_COL_TILE), lambda j: (0, j)),
        ],
        out_specs=pl.BlockSpec((COND_ROWS, MOD_COL_TILE), lambda j: (0, j)),
        out_shape=jax.ShapeDtypeStruct((COND_ROWS, n_out), F32),
        compiler_params=_params(1),
        name="adaln_table",
    )(cond, w_mod, b_mod.reshape(1, n_out))
    return out[:N_COND].reshape(N_COND, N_MOD, D_MODEL)


def _in0_kernel(xp_ref, xs_ref, mod_ref, nw_ref, w_ref, *rest):
    n_cast = (len(rest) - 2) // 2
    cast_in, o_ref, cast_out, w_bf = rest[:n_cast], rest[n_cast], rest[n_cast + 1:-1], rest[-1]
    i = pl.program_id(0)

    @pl.when(i == 0)
    def _cast_own_weight():
        def body(r, carry):
            rows = pl.ds(pl.multiple_of(r * CHUNK, CHUNK), CHUNK)
            w_bf[rows, :] = w_ref[rows, :].astype(BF16)
            return carry
        lax.fori_loop(0, D_MODEL // CHUNK, body, 0)

    _cast_slabs(i, cast_in, cast_out)

    x = jnp.where(i < N_CTX_TILES, xp_ref[...], xs_ref[...])
    h = _rms_mod(x, nw_ref[...], mod_ref[0, 0:1, :], mod_ref[0, 1:2, :]).astype(BF16)

    def proj(lo, hi):
        return jnp.dot(h, w_bf[:, lo:hi], preferred_element_type=F32)

    q_lo, k_lo, v_lo, g_lo, u_lo, z_lo = 0, QK_W, 2 * QK_W, 2 * QK_W + V_W, 2 * QK_W + 2 * V_W, IN0_W - GMLP_W
    o_ref[:, q_lo:k_lo] = proj(q_lo, k_lo).astype(BF16)
    o_ref[:, k_lo:v_lo] = (proj(k_lo, v_lo) * (DK ** -0.5)).astype(BF16)
    o_ref[:, v_lo:g_lo] = proj(v_lo, g_lo).astype(BF16)
    o_ref[:, g_lo:u_lo] = jax.nn.silu(proj(g_lo, u_lo)).astype(BF16)
    o_ref[:, u_lo:z_lo] = jax.nn.gelu(proj(u_lo, z_lo)).astype(BF16)
    z = jax.nn.gelu(proj(z_lo, IN0_W))
    zc = z - jnp.mean(z, axis=-1, keepdims=True)
    o_ref[:, z_lo:IN0_W] = (zc * lax.rsqrt(jnp.mean(zc * zc, axis=-1, keepdims=True) + EPS)).astype(BF16)


def _x_specs(tile_of_step):
    return [
        pl.BlockSpec((TILE, D_MODEL), lambda s: (jnp.minimum(tile_of_step(s), N_CTX_TILES - 1), 0)),
        pl.BlockSpec((TILE, D_MODEL), lambda s: (jnp.maximum(tile_of_step(s) - N_CTX_TILES, 0), 0)),
    ]


def _in0(xp, xs, mod, norm_w, w_in, cast_weights):
    return pl.pallas_call(
        _in0_kernel,
        grid=(N_TILES,),
        in_specs=_x_specs(lambda s: s) + [
            pl.BlockSpec((1, N_MOD, D_MODEL), lambda i: (_tile_cond(i), 0, 0)),
            _resident((1, D_MODEL)),
            _resident((D_MODEL, IN0_W)),
        ] + _cast_in_specs(cast_weights),
        out_specs=[pl.BlockSpec((TILE, IN0_W), lambda i: (i, 0))] + _cast_in_specs(cast_weights),
        out_shape=[jax.ShapeDtypeStruct((N_TOK, IN0_W), BF16)] + _cast_out_shapes(cast_weights),
        scratch_shapes=[pltpu.VMEM((D_MODEL, IN0_W), BF16)],
        compiler_params=_params(1),
        name="l0_in_proj",
    )(xp, xs, mod, norm_w.reshape(1, D_MODEL), w_in, *cast_weights)


def _build_decays(lg_ref, d, q_decay, k_decay, chunk_decay):
    pos = lax.broadcasted_iota(jnp.int32, (CHUNK, 1), 0)
    pos = (pos if d == 0 else CHUNK - 1 - pos).astype(F32)
    for h in range(N_HEADS):
        lg = lg_ref[d, h]
        if q_decay is not None:
            q_decay[d, h] = jnp.exp((pos + 1.0) * lg)
        k_decay[d, h] = jnp.exp((CHUNK - 1.0 - pos) * lg)
        chunk_decay[d, h] = jnp.exp(jnp.full((1, DV), CHUNK, F32) * lg)


def _state_update(s, k, v, k_decay, chunk_decay):
    k_dec = (k.astype(F32) * k_decay).astype(BF16)
    return chunk_decay * s + _dot_tn(k_dec, v)


def _lat_state_kernel(lg_ref, k_ref, v_ref, s0_ref, sin_ref, state, k_decay, chunk_decay):
    d = pl.program_id(0)
    b = pl.program_id(1)
    t = pl.program_id(2)

    @pl.when((b == 0) & (t == 0) & (d == 0))
    def _():
        _build_decays(lg_ref, 0, None, k_decay, chunk_decay)

    @pl.when((b == 0) & (t == 0) & (d == 1))
    def _():
        _build_decays(lg_ref, 1, None, k_decay, chunk_decay)

    @pl.when(t == 0)
    def _():
        state[...] = s0_ref[0, 0]

    sin_ref[0, 0] = state[...]
    for step in range(TILE // CHUNK):
        chunk = jnp.where(d == 0, step, TILE // CHUNK - 1 - step)
        rows = pl.ds(pl.multiple_of(chunk * CHUNK, CHUNK), CHUNK)
        for h in range(N_HEADS):
            state[h] = _state_update(state[h], k_ref[rows, h * DK:(h + 1) * DK], v_ref[rows, h * DV:(h + 1) * DV],
                                     k_decay[d, h], chunk_decay[d, h])


def _lat_tile(d, b, t):
    return b * LAT_TILES_PER_SEQ + jnp.where(d == 0, t, LAT_TILES_PER_SEQ - 1 - t)


def _lat_states(log_gamma, proj, s0):
    return pl.pallas_call(
        _lat_state_kernel,
        grid=(2, N_LAT_SEQ, LAT_TILES_PER_SEQ),
        in_specs=[
            pl.BlockSpec(memory_space=pltpu.SMEM),
            pl.BlockSpec((TILE, QK_W), lambda d, b, t: (N_CTX_TILES + _lat_tile(d, b, t), 1)),
            pl.BlockSpec((TILE, V_W), lambda d, b, t: (N_CTX_TILES + _lat_tile(d, b, t), 1)),
            pl.BlockSpec((1, 1, N_HEADS, DK, DV), lambda d, b, t: (b, d, 0, 0, 0)),
        ],
        out_specs=pl.BlockSpec((1, 1, N_HEADS, DK, DV), lambda d, b, t: (_lat_tile(d, b, t), d, 0, 0, 0)),
        out_shape=jax.ShapeDtypeStruct((N_LAT_TILES, 2, N_HEADS, DK, DV), F32),
        scratch_shapes=[
            pltpu.VMEM((N_HEADS, DK, DV), F32),
            pltpu.VMEM((2, N_HEADS, CHUNK, 1), F32),
            pltpu.VMEM((2, N_HEADS, 1, DV), F32),
        ],
        compiler_params=_params(3),
        name="l0_latent_states",
    )(log_gamma, proj, proj, s0)


def _mlp_residual(x, mod_ref, nw_ref, w1_ref, w2_ref):
    h = _rms_mod(x, nw_ref[...], mod_ref[0, 3:4, :], mod_ref[0, 4:5, :])
    a = jnp.maximum(_dot(h, w1_ref[...]), 0.0)
    return x + mod_ref[0, 5:6, :] * _dot(a * a, w2_ref[...])


def _out0_tile(step):
    return jnp.where(step < N_LAT_TILES, step + N_CTX_TILES, step - N_LAT_TILES)


def _out0_kernel(lg_ref, xp_ref, xs_ref, qk_ref, v_ref, g_ref, uz_ref, sin_ref, mod_ref, nw_ref, ws_ref, gb_ref,
                 wout_ref, w1_ref, w2_ref, *rest):
    n_cast = (len(rest) - 6) // 2
    cast_in, y_ref, sfin_ref = rest[:n_cast], rest[n_cast], rest[n_cast + 1]
    cast_out = rest[n_cast + 2:2 * n_cast + 2]
    mask, q_decay, k_decay, chunk_decay = rest[2 * n_cast + 2:]
    step = pl.program_id(0)
    is_ctx = step >= N_LAT_TILES

    @pl.when(step == 0)
    def _():
        row = lax.broadcasted_iota(jnp.int32, (CHUNK, CHUNK), 0)
        col = lax.broadcasted_iota(jnp.int32, (CHUNK, CHUNK), 1)
        diff = (row - col).astype(F32)
        for h in range(N_HEADS):
            fwd = jnp.where(diff >= 0, jnp.exp(jnp.maximum(diff, 0.0) * lg_ref[0, h]), 0.0)
            bwd = jnp.where(-diff >= 0, jnp.exp(jnp.maximum(-diff, 0.0) * lg_ref[1, h]), 0.0)
            mask[h] = fwd + bwd
        _build_decays(lg_ref, 0, q_decay, k_decay, chunk_decay)
        _build_decays(lg_ref, 1, q_decay, k_decay, chunk_decay)

    _cast_slabs(step, cast_in, cast_out)

    x = jnp.where(is_ctx, xp_ref[...], xs_ref[...])
    lo, hi = slice(0, CHUNK), slice(CHUNK, TILE)
    mix = jnp.zeros((TILE, D_MODEL), F32)
    for h in range(N_HEADS):
        qc = slice(h * DK, (h + 1) * DK)
        kc = slice(QK_W + h * DK, QK_W + (h + 1) * DK)
        vc = slice(h * DV, (h + 1) * DV)
        q_lo, q_hi = qk_ref[lo, qc], qk_ref[hi, qc]
        k_lo, k_hi = qk_ref[lo, kc], qk_ref[hi, kc]
        v_lo, v_hi = v_ref[lo, vc], v_ref[hi, vc]
        s_f = jnp.where(is_ctx, 0.0, sin_ref[0, 0, h])
        s_b = jnp.where(is_ctx, 0.0, sin_ref[0, 1, h])
        inner_lo = jnp.dot((_dot_nt(q_lo, k_lo) * mask[h]).astype(BF16), v_lo, preferred_element_type=F32)
        inner_hi = jnp.dot((_dot_nt(q_hi, k_hi) * mask[h]).astype(BF16), v_hi, preferred_element_type=F32)
        cross_f_lo = jnp.dot(q_lo, s_f.astype(BF16), preferred_element_type=F32) * q_decay[0, h]
        s_f = _state_update(s_f, k_lo, v_lo, k_decay[0, h], chunk_decay[0, h])
        cross_f_hi = jnp.dot(q_hi, s_f.astype(BF16), preferred_element_type=F32) * q_decay[0, h]
        s_f = _state_update(s_f, k_hi, v_hi, k_decay[0, h], chunk_decay[0, h])
        cross_b_hi = jnp.dot(q_hi, s_b.astype(BF16), preferred_element_type=F32) * q_decay[1, h]
        s_b = _state_update(s_b, k_hi, v_hi, k_decay[1, h], chunk_decay[1, h])
        cross_b_lo = jnp.dot(q_lo, s_b.astype(BF16), preferred_element_type=F32) * q_decay[1, h]
        s_b = _state_update(s_b, k_lo, v_lo, k_decay[1, h], chunk_decay[1, h])
        sfin_ref[0, 0, h] = s_f
        sfin_ref[0, 1, h] = s_b
        o = jnp.concatenate([inner_lo + cross_f_lo + cross_b_lo, inner_hi + cross_f_hi + cross_b_hi], axis=0)
        o = o * lax.rsqrt(jnp.mean(o * o, axis=-1, keepdims=True) + EPS)
        mix += _dot(g_ref[:, vc].astype(F32) * o, wout_ref[h * DV:(h + 1) * DV, :])

    rows = []
    for n in range(TILE // CHUNK):
        groups = []
        for g in range(N_GROUPS):
            zc = uz_ref[n * CHUNK:(n + 1) * CHUNK, GMLP_W + g * GMLP_CH:GMLP_W + (g + 1) * GMLP_CH]
            uc = uz_ref[n * CHUNK:(n + 1) * CHUNK, g * GMLP_CH:(g + 1) * GMLP_CH]
            sv = jnp.dot(ws_ref[g], zc, preferred_element_type=F32) + gb_ref[:, g:g + 1]
            groups.append(uc.astype(F32) * sv)
        rows.append(jnp.concatenate(groups, axis=-1))
    mix += _dot(jnp.concatenate(rows, axis=0), wout_ref[V_W:, :])

    x1 = x + mod_ref[0, 2:3, :] * mix
    y_ref[...] = _mlp_residual(x1, mod_ref, nw_ref, w1_ref, w2_ref)


def _out0(log_gamma, xp, xs, proj, sin, mod, norm_w, gmlp_ws, gmlp_b_t, w_out, w1, w2, cast_weights):
    state_block = (1, 2, N_HEADS, DK, DV)
    return pl.pallas_call(
        _out0_kernel,
        grid=(N_TILES,),
        in_specs=[pl.BlockSpec(memory_space=pltpu.SMEM)] + _x_specs(_out0_tile) + [
            pl.BlockSpec((TILE, 2 * QK_W), lambda s: (_out0_tile(s), 0)),
            pl.BlockSpec((TILE, V_W), lambda s: (_out0_tile(s), 1)),
            pl.BlockSpec((TILE, V_W), lambda s: (_out0_tile(s), 2)),
            pl.BlockSpec((TILE, 2 * GMLP_W), lambda s: (_out0_tile(s), 3)),
            pl.BlockSpec(state_block, lambda s: (jnp.minimum(s, N_LAT_TILES - 1), 0, 0, 0, 0)),
            pl.BlockSpec((1, N_MOD, D_MODEL), lambda s: (_tile_cond(_out0_tile(s)), 0, 0)),
            _resident((1, D_MODEL)),
            _resident((N_GROUPS, CHUNK, CHUNK)),
            _resident((CHUNK, N_GROUPS)),
            _resident((OUT0_W, D_MODEL)),
            _resident((D_MODEL, D_FF)),
            _resident((D_FF, D_MODEL)),
        ] + _cast_in_specs(cast_weights),
        out_specs=[
            pl.BlockSpec((TILE, D_MODEL), lambda s: (_out0_tile(s), 0)),
            pl.BlockSpec(state_block, lambda s: (jnp.maximum(s - N_LAT_TILES, 0), 0, 0, 0, 0)),
        ] + _cast_in_specs(cast_weights),
        out_shape=[jax.ShapeDtypeStruct((N_TOK, D_MODEL), F32),
                   jax.ShapeDtypeStruct((N_CTX_SEQ, 2, N_HEADS, DK, DV), F32)] + _cast_out_shapes(cast_weights),
        scratch_shapes=[
            pltpu.VMEM((N_HEADS, CHUNK, CHUNK), F32),
            pltpu.VMEM((2, N_HEADS, CHUNK, 1), F32),
            pltpu.VMEM((2, N_HEADS, CHUNK, 1), F32),
            pltpu.VMEM((2, N_HEADS, 1, DV), F32),
        ],
        compiler_params=_params(1),
        name="l0_mix_out_mlp",
    )(log_gamma, xp, xs, proj, proj, proj, proj, sin, mod, norm_w.reshape(1, D_MODEL), gmlp_ws, gmlp_b_t,
      w_out, w1, w2, *cast_weights)


def _l1_kernel(x_ref, mod_ref, n1_ref, win_ref, cw_ref, wout_ref, n2_ref, w1_ref, w2_ref, fn_ref, yp_ref, ys_ref):
    i = pl.program_id(0)
    is_ctx = i < N_CTX_TILES
    x = x_ref[...]
    h = _rms_mod(x, n1_ref[...], mod_ref[0, 0:1, :], mod_ref[0, 1:2, :]).astype(BF16)
    bg = jnp.dot(h, win_ref[:, 0:D_MODEL], preferred_element_type=F32)
    cg = jnp.dot(h, win_ref[:, D_MODEL:2 * D_MODEL], preferred_element_type=F32)
    hv = jnp.dot(h, win_ref[:, 2 * D_MODEL:3 * D_MODEL], preferred_element_type=F32)
    xc = cg * hv
    period = jnp.where(is_ctx, CTX_LEN, GRID_W)
    t = lax.broadcasted_iota(jnp.int32, (TILE, 1), 0) & (period - 1)
    prev = jnp.where(t == 0, 0.0, pltpu.roll(xc, 1, 0))
    nxt = jnp.where(t == period - 1, 0.0, pltpu.roll(xc, TILE - 1, 0))
    yc = cw_ref[0:1, :] * prev + cw_ref[1:2, :] * xc + cw_ref[2:3, :] * nxt
    x1 = x + mod_ref[0, 2:3, :] * _dot(bg * yc, wout_ref[...])
    x2 = _mlp_residual(x1, mod_ref, n2_ref, w1_ref, w2_ref)
    y = (x2 * lax.rsqrt(jnp.mean(x2 * x2, axis=-1, keepdims=True) + EPS)) * fn_ref[...]

    @pl.when(is_ctx)
    def _():
        yp_ref[...] = y

    @pl.when(jnp.logical_not(is_ctx))
    def _():
        ys_ref[...] = y


def _l1(x, mod, norm1, w_in, conv_w, w_out, norm2, w1, w2, final_norm):
    return pl.pallas_call(
        _l1_kernel,
        grid=(N_TILES,),
        in_specs=[
            pl.BlockSpec((TILE, D_MODEL), lambda i: (i, 0)),
            pl.BlockSpec((1, N_MOD, D_MODEL), lambda i: (_tile_cond(i), 0, 0)),
            _resident((1, D_MODEL)),
            _resident((D_MODEL, 3 * D_MODEL)),
            _resident((3, D_MODEL)),
            _resident((D_MODEL, D_MODEL)),
            _resident((1, D_MODEL)),
            _resident((D_MODEL, D_FF)),
            _resident((D_FF, D_MODEL)),
            _resident((1, D_MODEL)),
        ],
        out_specs=[
            pl.BlockSpec((TILE, D_MODEL), lambda i: (jnp.minimum(i, N_CTX_TILES - 1), 0)),
            pl.BlockSpec((TILE, D_MODEL), lambda i: (jnp.maximum(i - N_CTX_TILES, 0), 0)),
        ],
        out_shape=[jax.ShapeDtypeStruct((N_CTX_TOK, D_MODEL), F32),
                   jax.ShapeDtypeStruct((N_LAT_TOK, D_MODEL), F32)],
        compiler_params=_params(1),
        name="l1_conv_mlp_norm",
    )(x, mod, norm1.reshape(1, D_MODEL), w_in, conv_w, w_out, norm2.reshape(1, D_MODEL), w1, w2,
      final_norm.reshape(1, D_MODEL))


def kernel(x_prompt, x_sample, state_l0_ret, c, c_ctx, l0_norm1, l0_w_in, l0_ret_decay_exp, l0_gmlp_ws,
           l0_gmlp_b, l0_w_out, l0_norm2, l0_w_mod, l0_b_mod, l0_ffn_w1, l0_ffn_w2, l1_norm1, l1_w_in,
           l1_conv_w, l1_w_out, l1_norm2, l1_w_mod, l1_b_mod, l1_ffn_w1, l1_ffn_w2, final_norm):
    xp = x_prompt.reshape(N_CTX_TOK, D_MODEL)
    xs = x_sample.reshape(N_LAT_TOK, D_MODEL)
    cond = jnp.concatenate([c_ctx[None, :], c, jnp.zeros((COND_ROWS - N_COND, D_MODEL), F32)], axis=0)
    log_gamma = jnp.log1p(-jnp.exp2(-l0_ret_decay_exp.astype(F32)))

    mod0 = _mod_table(cond, l0_w_mod, l0_b_mod)
    mod1 = _mod_table(cond, l1_w_mod, l1_b_mod)

    proj, w_out0, w1_0, w2_0 = _in0(xp, xs, mod0, l0_norm1, l0_w_in, [l0_w_out, l0_ffn_w1, l0_ffn_w2])
    sin = _lat_states(log_gamma, proj, state_l0_ret)
    x1, new_state, w_in1, w_out1, w1_1, w2_1 = _out0(
        log_gamma, xp, xs, proj, sin, mod0, l0_norm2, l0_gmlp_ws.astype(BF16), l0_gmlp_b.T,
        w_out0, w1_0, w2_0, [l1_w_in, l1_w_out, l1_ffn_w1, l1_ffn_w2])
    yp, ys = _l1(x1, mod1, l1_norm1, w_in1, l1_conv_w, w_out1, l1_norm2, w1_1, w2_1, final_norm)

    return (yp.reshape(N_CTX_SEQ, CTX_LEN, D_MODEL), ys.reshape(N_LAT_SEQ, LAT_LEN, D_MODEL), new_state)
```

```python
import jax
import jax.numpy as jnp
from jax import lax
from jax.experimental import pallas as pl
from jax.experimental.pallas import tpu as pltpu

F32 = jnp.float32
BF16 = jnp.bfloat16

D_MODEL = 1024
N_CTX_SEQ = 16
CTX_LEN = 256
N_LAT_SEQ = 2
LAT_LEN = 1024
GRID_W = 64
N_CTX_TOK = N_CTX_SEQ * CTX_LEN
N_LAT_TOK = N_LAT_SEQ * LAT_LEN
N_TOK = N_CTX_TOK + N_LAT_TOK
N_HEADS = 4
DK = 128
DV = 256
CHUNK = 128
N_GROUPS = 4
GMLP_CH = 128
QK_W = N_HEADS * DK
V_W = N_HEADS * DV
GMLP_W = N_GROUPS * GMLP_CH
IN0_W = 2 * QK_W + 2 * V_W + 2 * GMLP_W
OUT0_W = V_W + GMLP_W
D_FF = 4 * D_MODEL
N_MOD = 6
MOD_W = N_MOD * D_MODEL
EPS = 1e-6
N_COND = 3
COND_ROWS = 16

TILE = 256
CHUNKS_PER_TILE = TILE // CHUNK
N_TILES = N_TOK // TILE
N_CTX_TILES = N_CTX_TOK // TILE
N_LAT_TILES = N_LAT_TOK // TILE
LAT_TILES_PER_SEQ = LAT_LEN // TILE
N_CAST_SLABS = 16
MOD_COL_TILE = 1536
VMEM_LIMIT_BYTES = 56 * 1024 * 1024


def _params(n_axes):
    return pltpu.CompilerParams(dimension_semantics=("arbitrary",) * n_axes,
                                vmem_limit_bytes=VMEM_LIMIT_BYTES)


def _resident(shape):
    return pl.BlockSpec(shape, lambda *_: (0,) * len(shape), pipeline_mode=pl.Buffered(1))


def _tile_cond(tile):
    return jnp.where(tile < N_CTX_TILES, 0, 1 + (tile - N_CTX_TILES) // LAT_TILES_PER_SEQ)


def _stage_a(step):
    return jnp.minimum(step, N_TILES - 1)


def _stage_b(step):
    return jnp.maximum(step - 1, 0)


def _lat_first(pos):
    return jnp.where(pos < N_LAT_TILES, pos + N_CTX_TILES, pos - N_LAT_TILES)


def _x_specs(tile_of_step):
    return [
        pl.BlockSpec((TILE, D_MODEL), lambda s: (jnp.minimum(tile_of_step(s), N_CTX_TILES - 1), 0)),
        pl.BlockSpec((TILE, D_MODEL), lambda s: (jnp.maximum(tile_of_step(s) - N_CTX_TILES, 0), 0)),
    ]


def _mod_spec(tile_of_step):
    return pl.BlockSpec((1, N_MOD, D_MODEL), lambda s: (_tile_cond(tile_of_step(s)), 0, 0))


def _rms(x):
    return x * lax.rsqrt(jnp.mean(x * x, axis=-1, keepdims=True) + EPS)


def _rms_mod(x, norm_w, shift, scale):
    return (_rms(x) * norm_w) * (1.0 + scale) + shift


def _dot(a, b):
    return jnp.dot(a.astype(BF16), b.astype(BF16), preferred_element_type=F32)


def _dot_nt(a, b):
    return lax.dot_general(a, b, (((1,), (1,)), ((), ())), preferred_element_type=F32)


def _dot_tn(a, b):
    return lax.dot_general(a, b, (((0,), (0,)), ((), ())), preferred_element_type=F32)


def _cast_in_specs(weights):
    return [pl.BlockSpec((w.shape[0] // N_CAST_SLABS, w.shape[1]),
                         lambda s: (jnp.minimum(s, N_CAST_SLABS - 1), 0)) for w in weights]


def _cast_out_shapes(weights):
    return [jax.ShapeDtypeStruct(w.shape, BF16) for w in weights]


def _cast_slabs(step, in_refs, out_refs):
    @pl.when(step < N_CAST_SLABS)
    def _():
        for src, dst in zip(in_refs, out_refs):
            dst[...] = src[...].astype(BF16)


def _mod_kernel(cond_ref, w_ref, b_ref, o_ref):
    o_ref[...] = _dot(jax.nn.silu(cond_ref[...]), w_ref[...]) + b_ref[...]


def _mod_rows(raw):
    return raw[:N_COND].reshape(N_COND, N_MOD, D_MODEL)


def _mod_table(cond, w_mod, b_mod):
    return pl.pallas_call(
        _mod_kernel,
        grid=(MOD_W // MOD_COL_TILE,),
        in_specs=[
            pl.BlockSpec((COND_ROWS, D_MODEL), lambda j: (0, 0)),
            pl.BlockSpec((D_MODEL, MOD_COL_TILE), lambda j: (0, j)),
            pl.BlockSpec((1, MOD_COL_TILE), lambda j: (0, j)),
        ],
        out_specs=pl.BlockSpec((COND_ROWS, MOD_COL_TILE), lambda j: (0, j)),
        out_shape=jax.ShapeDtypeStruct((COND_ROWS, MOD_W), F32),
        compiler_params=_params(1),
        name="adaln_table",
    )(cond, w_mod, b_mod.reshape(1, MOD_W))


def _in0_kernel(xp_ref, xs_ref, mod_ref, nw_ref, w_ref, cond_ref, wmod_ref, bmod_ref, *rest):
    n_cast = (len(rest) - 4) // 2
    cast_in, o_ref, mod_out_ref = rest[:n_cast], rest[n_cast], rest[n_cast + 1]
    cast_out, w_bf, h_prev = rest[n_cast + 2:-2], rest[-2], rest[-1]
    step = pl.program_id(0)

    @pl.when(step == 0)
    def _first_step():
        def body(r, carry):
            rows = pl.ds(pl.multiple_of(r * CHUNK, CHUNK), CHUNK)
            w_bf[rows, :] = w_ref[rows, :].astype(BF16)
            return carry
        lax.fori_loop(0, D_MODEL // CHUNK, body, 0)
        h_prev[...] = jnp.zeros_like(h_prev)

    _cast_slabs(step, cast_in, cast_out)
    mod_out_ref[...] = _dot(jax.nn.silu(cond_ref[...]), wmod_ref[...]) + bmod_ref[...]

    h = h_prev[...]

    def proj(lo, hi):
        return jnp.dot(h, w_bf[:, lo:hi], preferred_element_type=F32)

    q_lo, k_lo, v_lo, g_lo, u_lo, z_lo = 0, QK_W, 2 * QK_W, 2 * QK_W + V_W, 2 * QK_W + 2 * V_W, IN0_W - GMLP_W
    z = jax.nn.gelu(proj(z_lo, IN0_W))
    zc = z - jnp.mean(z, axis=-1, keepdims=True)
    o_ref[:, z_lo:IN0_W] = (zc * lax.rsqrt(jnp.mean(zc * zc, axis=-1, keepdims=True) + EPS)).astype(BF16)
    o_ref[:, u_lo:z_lo] = jax.nn.gelu(proj(u_lo, z_lo)).astype(BF16)
    x = jnp.where(step < N_CTX_TILES, xp_ref[...], xs_ref[...])
    h_prev[...] = _rms_mod(x, nw_ref[...], mod_ref[0, 0:1, :], mod_ref[0, 1:2, :]).astype(BF16)
    o_ref[:, g_lo:u_lo] = jax.nn.silu(proj(g_lo, u_lo)).astype(BF16)
    o_ref[:, k_lo:v_lo] = (proj(k_lo, v_lo) * (DK ** -0.5)).astype(BF16)
    o_ref[:, q_lo:k_lo] = proj(q_lo, k_lo).astype(BF16)
    o_ref[:, v_lo:g_lo] = proj(v_lo, g_lo).astype(BF16)


def _in0(xp, xs, mod, norm_w, w_in, cond, w_mod_next, b_mod_next, cast_weights):
    mod_cols = MOD_W // N_TILES
    return pl.pallas_call(
        _in0_kernel,
        grid=(N_TILES + 1,),
        in_specs=_x_specs(_stage_a) + [
            _mod_spec(_stage_a),
            _resident((1, D_MODEL)),
            _resident((D_MODEL, IN0_W)),
            _resident((COND_ROWS, D_MODEL)),
            pl.BlockSpec((D_MODEL, mod_cols), lambda s: (0, _stage_a(s))),
            pl.BlockSpec((1, mod_cols), lambda s: (0, _stage_a(s))),
        ] + _cast_in_specs(cast_weights),
        out_specs=[
            pl.BlockSpec((TILE, IN0_W), lambda s: (_stage_b(s), 0)),
            pl.BlockSpec((COND_ROWS, mod_cols), lambda s: (0, _stage_a(s))),
        ] + _cast_in_specs(cast_weights),
        out_shape=[jax.ShapeDtypeStruct((N_TOK, IN0_W), BF16),
                   jax.ShapeDtypeStruct((COND_ROWS, MOD_W), F32)] + _cast_out_shapes(cast_weights),
        scratch_shapes=[pltpu.VMEM((D_MODEL, IN0_W), BF16),
                        pltpu.VMEM((TILE, D_MODEL), BF16)],
        compiler_params=_params(1),
        name="l0_in_proj",
    )(xp, xs, mod, norm_w.reshape(1, D_MODEL), w_in, cond, w_mod_next, b_mod_next.reshape(1, MOD_W),
      *cast_weights)


def _build_decays(lg_ref, q_decay, k_decay, tile_decay):
    pos = lax.broadcasted_iota(jnp.int32, (TILE, 1), 0)
    for d in range(2):
        p = (pos if d == 0 else TILE - 1 - pos).astype(F32)
        for h in range(N_HEADS):
            lg = lg_ref[d, h]
            if q_decay is not None:
                q_decay[d, h] = jnp.exp((p + 1.0) * lg)
            k_decay[d, h] = jnp.exp((TILE - 1.0 - p) * lg)
            tile_decay[d, h] = jnp.exp(jnp.full((1, DV), TILE, F32) * lg)


def _state_update(s, k, v, k_decay, tile_decay):
    k_dec = (k.astype(F32) * k_decay).astype(BF16)
    return tile_decay * s + _dot_tn(k_dec, v)


def _lat_state_kernel(lg_ref, *refs):
    n_scan = 2 * N_LAT_SEQ
    k_refs, v_refs = refs[:n_scan], refs[n_scan:2 * n_scan]
    s0_ref, sin_f_ref, sin_b_ref, state, k_decay, tile_decay = refs[2 * n_scan:]
    t = pl.program_id(0)

    @pl.when(t == 0)
    def _():
        _build_decays(lg_ref, None, k_decay, tile_decay)
        state[...] = s0_ref[...]

    sin_f_ref[:, 0] = state[:, 0]
    sin_b_ref[:, 0] = state[:, 1]
    for b in range(N_LAT_SEQ):
        for d in range(2):
            k_ref, v_ref = k_refs[2 * b + d], v_refs[2 * b + d]
            for h in range(N_HEADS):
                state[b, d, h] = _state_update(state[b, d, h], k_ref[:, h * DK:(h + 1) * DK],
                                               v_ref[:, h * DV:(h + 1) * DV], k_decay[d, h], tile_decay[d, h])


def _lat_states(log_gamma, proj, s0):
    def tile(b, d):
        return lambda t: (N_CTX_TILES + b * LAT_TILES_PER_SEQ + (t if d == 0 else LAT_TILES_PER_SEQ - 1 - t), 1)

    scans = [(b, d) for b in range(N_LAT_SEQ) for d in range(2)]
    state_block = (N_LAT_SEQ, 1, N_HEADS, DK, DV)
    shape = jax.ShapeDtypeStruct((N_LAT_SEQ, LAT_TILES_PER_SEQ, N_HEADS, DK, DV), F32)
    sin_f, sin_b = pl.pallas_call(
        _lat_state_kernel,
        grid=(LAT_TILES_PER_SEQ,),
        in_specs=[pl.BlockSpec(memory_space=pltpu.SMEM)]
        + [pl.BlockSpec((TILE, QK_W), tile(b, d)) for b, d in scans]
        + [pl.BlockSpec((TILE, V_W), tile(b, d)) for b, d in scans]
        + [_resident((N_LAT_SEQ, 2, N_HEADS, DK, DV))],
        out_specs=[pl.BlockSpec(state_block, lambda t: (0, t, 0, 0, 0)),
                   pl.BlockSpec(state_block, lambda t: (0, LAT_TILES_PER_SEQ - 1 - t, 0, 0, 0))],
        out_shape=[shape, shape],
        scratch_shapes=[
            pltpu.VMEM((N_LAT_SEQ, 2, N_HEADS, DK, DV), F32),
            pltpu.VMEM((2, N_HEADS, TILE, 1), F32),
            pltpu.VMEM((2, N_HEADS, 1, DV), F32),
        ],
        compiler_params=_params(1),
        name="l0_latent_states",
    )(log_gamma, *([proj] * (2 * len(scans))), s0)
    return (sin_f.reshape(N_LAT_TILES, N_HEADS, DK, DV), sin_b.reshape(N_LAT_TILES, N_HEADS, DK, DV))


N_FF_BLOCKS = 4
FF_BLOCK = D_FF // N_FF_BLOCKS


def _mlp_in(x, mod_ref, nw_ref):
    return _rms_mod(x, nw_ref[...], mod_ref[0, 3:4, :], mod_ref[0, 4:5, :]).astype(BF16)


def _mlp_block(h, w1_ref, w2_ref, j):
    cols = slice(j * FF_BLOCK, (j + 1) * FF_BLOCK)
    a = jnp.maximum(jnp.dot(h, w1_ref[:, cols], preferred_element_type=F32), 0.0)
    return jnp.dot((a * a).astype(BF16), w2_ref[cols, :], preferred_element_type=F32)


def _out0_kernel(lg_ref, xp_ref, xs_ref, qk_ref, v_ref, g_ref, uz_ref, sf_ref, sb_ref, mod_a_ref, mod_b_ref,
                 nw_ref, ws_ref, gb_ref, wout_ref, w1_ref, w2_ref, *rest):
    n_cast = (len(rest) - 8) // 2
    cast_in, y_ref, sfin_ref = rest[:n_cast], rest[n_cast], rest[n_cast + 1]
    cast_out = rest[n_cast + 2:2 * n_cast + 2]
    mask, q_decay, k_decay, tile_decay, x1_buf, hb_buf = rest[2 * n_cast + 2:]
    step = pl.program_id(0)
    is_ctx = step >= N_LAT_TILES
    slot_a = step % 2
    slot_b = 1 - slot_a

    @pl.when(step == 0)
    def _():
        row = lax.broadcasted_iota(jnp.int32, (TILE, TILE), 0)
        col = lax.broadcasted_iota(jnp.int32, (TILE, TILE), 1)
        diff = (row - col).astype(F32)
        for h in range(N_HEADS):
            fwd = jnp.where(diff >= 0, jnp.exp(jnp.maximum(diff, 0.0) * lg_ref[0, h]), 0.0)
            bwd = jnp.where(-diff >= 0, jnp.exp(jnp.maximum(-diff, 0.0) * lg_ref[1, h]), 0.0)
            mask[h] = fwd + bwd
        _build_decays(lg_ref, q_decay, k_decay, tile_decay)
        x1_buf[...] = jnp.zeros_like(x1_buf)
        hb_buf[...] = jnp.zeros_like(hb_buf)

    _cast_slabs(step, cast_in, cast_out)

    h_b = hb_buf[slot_b]
    mlp = jnp.zeros((TILE, D_MODEL), F32)
    mix = jnp.zeros((TILE, D_MODEL), F32)
    for h in range(N_HEADS):
        q = qk_ref[:, h * DK:(h + 1) * DK]
        k = qk_ref[:, QK_W + h * DK:QK_W + (h + 1) * DK]
        v = v_ref[:, h * DV:(h + 1) * DV]
        s_f = jnp.where(is_ctx, 0.0, sf_ref[0, h])
        s_b = jnp.where(is_ctx, 0.0, sb_ref[0, h])
        inner = jnp.dot((_dot_nt(q, k) * mask[h]).astype(BF16), v, preferred_element_type=F32)
        cross_f = jnp.dot(q, s_f.astype(BF16), preferred_element_type=F32) * q_decay[0, h]
        cross_b = jnp.dot(q, s_b.astype(BF16), preferred_element_type=F32) * q_decay[1, h]
        sfin_ref[0, 0, h] = _state_update(s_f, k, v, k_decay[0, h], tile_decay[0, h])
        sfin_ref[0, 1, h] = _state_update(s_b, k, v, k_decay[1, h], tile_decay[1, h])
        o = inner + cross_f + cross_b
        mix += _dot(g_ref[:, h * DV:(h + 1) * DV].astype(F32) * _rms(o), wout_ref[h * DV:(h + 1) * DV, :])
        mlp += _mlp_block(h_b, w1_ref, w2_ref, h)

    y_ref[...] = x1_buf[slot_b] + mod_b_ref[0, 5:6, :] * mlp
    rows = []
    for n in range(CHUNKS_PER_TILE):
        groups = []
        for g in range(N_GROUPS):
            zc = uz_ref[n * CHUNK:(n + 1) * CHUNK, GMLP_W + g * GMLP_CH:GMLP_W + (g + 1) * GMLP_CH]
            uc = uz_ref[n * CHUNK:(n + 1) * CHUNK, g * GMLP_CH:(g + 1) * GMLP_CH]
            sv = jnp.dot(ws_ref[g], zc, preferred_element_type=F32) + gb_ref[:, g:g + 1]
            groups.append(uc.astype(F32) * sv)
        rows.append(jnp.concatenate(groups, axis=-1))
    mix += _dot(jnp.concatenate(rows, axis=0), wout_ref[V_W:, :])

    x1 = jnp.where(is_ctx, xp_ref[...], xs_ref[...]) + mod_a_ref[0, 2:3, :] * mix
    x1_buf[slot_a] = x1
    hb_buf[slot_a] = _mlp_in(x1, mod_a_ref, nw_ref)


def _out0(log_gamma, xp, xs, proj, sin_f, sin_b, mod, norm_w, gmlp_ws, gmlp_b_t, w_out, w1, w2, cast_weights):
    def tile_a(s):
        return _lat_first(_stage_a(s))

    def tile_b(s):
        return _lat_first(_stage_b(s))

    entry_spec = pl.BlockSpec((1, N_HEADS, DK, DV), lambda s: (jnp.minimum(s, N_LAT_TILES - 1), 0, 0, 0))
    return pl.pallas_call(
        _out0_kernel,
        grid=(N_TILES + 1,),
        in_specs=[pl.BlockSpec(memory_space=pltpu.SMEM)] + _x_specs(tile_a) + [
            pl.BlockSpec((TILE, 2 * QK_W), lambda s: (tile_a(s), 0)),
            pl.BlockSpec((TILE, V_W), lambda s: (tile_a(s), 1)),
            pl.BlockSpec((TILE, V_W), lambda s: (tile_a(s), 2)),
            pl.BlockSpec((TILE, 2 * GMLP_W), lambda s: (tile_a(s), 3)),
            entry_spec,
            entry_spec,
            _mod_spec(tile_a),
            _mod_spec(tile_b),
            _resident((1, D_MODEL)),
            _resident((N_GROUPS, CHUNK, CHUNK)),
            _resident((CHUNK, N_GROUPS)),
            _resident((OUT0_W, D_MODEL)),
            _resident((D_MODEL, D_FF)),
            _resident((D_FF, D_MODEL)),
        ] + _cast_in_specs(cast_weights),
        out_specs=[
            pl.BlockSpec((TILE, D_MODEL), lambda s: (tile_b(s), 0)),
            pl.BlockSpec((1, 2, N_HEADS, DK, DV),
                         lambda s: (jnp.maximum(_stage_a(s) - N_LAT_TILES, 0), 0, 0, 0, 0)),
        ] + _cast_in_specs(cast_weights),
        out_shape=[jax.ShapeDtypeStruct((N_TOK, D_MODEL), F32),
                   jax.ShapeDtypeStruct((N_CTX_SEQ, 2, N_HEADS, DK, DV), F32)] + _cast_out_shapes(cast_weights),
        scratch_shapes=[
            pltpu.VMEM((N_HEADS, TILE, TILE), F32),
            pltpu.VMEM((2, N_HEADS, TILE, 1), F32),
            pltpu.VMEM((2, N_HEADS, TILE, 1), F32),
            pltpu.VMEM((2, N_HEADS, 1, DV), F32),
            pltpu.VMEM((2, TILE, D_MODEL), F32),
            pltpu.VMEM((2, TILE, D_MODEL), BF16),
        ],
        compiler_params=_params(1),
        name="l0_mix_out_mlp",
    )(log_gamma, xp, xs, proj, proj, proj, proj, sin_f, sin_b, mod, mod, norm_w.reshape(1, D_MODEL), gmlp_ws,
      gmlp_b_t, w_out, w1, w2, *cast_weights)


def _l1_kernel(x_ref, mod_a_ref, mod_b_ref, n1_ref, win_ref, cw_ref, wout_ref, n2_ref, w1_ref, w2_ref, fn_ref,
               yp_ref, ys_ref, x1_buf, hb_buf):
    step = pl.program_id(0)
    slot_a = step % 2
    slot_b = 1 - slot_a

    @pl.when(step == 0)
    def _():
        x1_buf[...] = jnp.zeros_like(x1_buf)
        hb_buf[...] = jnp.zeros_like(hb_buf)

    h_b = hb_buf[slot_b]
    mlp = _mlp_block(h_b, w1_ref, w2_ref, 0)
    x = x_ref[...]
    h = _rms_mod(x, n1_ref[...], mod_a_ref[0, 0:1, :], mod_a_ref[0, 1:2, :]).astype(BF16)
    cg = jnp.dot(h, win_ref[:, D_MODEL:2 * D_MODEL], preferred_element_type=F32)
    hv = jnp.dot(h, win_ref[:, 2 * D_MODEL:3 * D_MODEL], preferred_element_type=F32)
    xc = cg * hv
    period = jnp.where(step >= N_LAT_TILES, CTX_LEN, GRID_W)
    t = lax.broadcasted_iota(jnp.int32, (TILE, 1), 0) & (period - 1)
    prev = jnp.where(t == 0, 0.0, pltpu.roll(xc, 1, 0))
    nxt = jnp.where(t == period - 1, 0.0, pltpu.roll(xc, TILE - 1, 0))
    yc = cw_ref[0:1, :] * prev + cw_ref[1:2, :] * xc + cw_ref[2:3, :] * nxt
    bg = jnp.dot(h, win_ref[:, 0:D_MODEL], preferred_element_type=F32)
    mlp += _mlp_block(h_b, w1_ref, w2_ref, 1)
    mlp += _mlp_block(h_b, w1_ref, w2_ref, 2)
    x1 = x + mod_a_ref[0, 2:3, :] * _dot(bg * yc, wout_ref[...])
    x1_buf[slot_a] = x1
    hb_buf[slot_a] = _mlp_in(x1, mod_a_ref, n2_ref)
    mlp += _mlp_block(h_b, w1_ref, w2_ref, 3)
    y = _rms(x1_buf[slot_b] + mod_b_ref[0, 5:6, :] * mlp) * fn_ref[...]
    yp_ref[...] = y
    ys_ref[...] = jnp.where(step <= N_LAT_TILES, y, ys_ref[...])


def _l1(x, mod, norm1, w_in, conv_w, w_out, norm2, w1, w2, final_norm):
    def tile_a(s):
        return _lat_first(_stage_a(s))

    def tile_b(s):
        return _lat_first(_stage_b(s))

    return pl.pallas_call(
        _l1_kernel,
        grid=(N_TILES + 1,),
        in_specs=[
            pl.BlockSpec((TILE, D_MODEL), lambda s: (tile_a(s), 0)),
            _mod_spec(tile_a),
            _mod_spec(tile_b),
            _resident((1, D_MODEL)),
            _resident((D_MODEL, 3 * D_MODEL)),
            _resident((3, D_MODEL)),
            _resident((D_MODEL, D_MODEL)),
            _resident((1, D_MODEL)),
            _resident((D_MODEL, D_FF)),
            _resident((D_FF, D_MODEL)),
            _resident((1, D_MODEL)),
        ],
        out_specs=[
            pl.BlockSpec((TILE, D_MODEL), lambda s: (jnp.maximum(_stage_b(s) - N_LAT_TILES, 0), 0)),
            pl.BlockSpec((TILE, D_MODEL), lambda s: (jnp.minimum(_stage_b(s), N_LAT_TILES - 1), 0)),
        ],
        out_shape=[jax.ShapeDtypeStruct((N_CTX_TOK, D_MODEL), F32),
                   jax.ShapeDtypeStruct((N_LAT_TOK, D_MODEL), F32)],
        scratch_shapes=[pltpu.VMEM((2, TILE, D_MODEL), F32),
                        pltpu.VMEM((2, TILE, D_MODEL), BF16)],
        compiler_params=_params(1),
        name="l1_conv_mlp_norm",
    )(x, mod, mod, norm1.reshape(1, D_MODEL), w_in, conv_w, w_out, norm2.reshape(1, D_MODEL), w1, w2,
      final_norm.reshape(1, D_MODEL))


def kernel(x_prompt, x_sample, state_l0_ret, c, c_ctx, l0_norm1, l0_w_in, l0_ret_decay_exp, l0_gmlp_ws,
           l0_gmlp_b, l0_w_out, l0_norm2, l0_w_mod, l0_b_mod, l0_ffn_w1, l0_ffn_w2, l1_norm1, l1_w_in,
           l1_conv_w, l1_w_out, l1_norm2, l1_w_mod, l1_b_mod, l1_ffn_w1, l1_ffn_w2, final_norm):
    xp = x_prompt.reshape(N_CTX_TOK, D_MODEL)
    xs = x_sample.reshape(N_LAT_TOK, D_MODEL)
    cond = jnp.concatenate([c_ctx[None, :], c, jnp.zeros((COND_ROWS - N_COND, D_MODEL), F32)], axis=0)
    log_gamma = jnp.log1p(-jnp.exp2(-l0_ret_decay_exp.astype(F32)))

    mod0 = _mod_rows(_mod_table(cond, l0_w_mod, l0_b_mod))
    proj, mod1_raw, w_out0, w1_0, w2_0 = _in0(xp, xs, mod0, l0_norm1, l0_w_in, cond, l1_w_mod, l1_b_mod,
                                              [l0_w_out, l0_ffn_w1, l0_ffn_w2])
    sin_f, sin_b = _lat_states(log_gamma, proj, state_l0_ret)
    x1, new_state, w_in1, w_out1, w1_1, w2_1 = _out0(
        log_gamma, xp, xs, proj, sin_f, sin_b, mod0, l0_norm2, l0_gmlp_ws.astype(BF16), l0_gmlp_b.T,
        w_out0, w1_0, w2_0, [l1_w_in, l1_w_out, l1_ffn_w1, l1_ffn_w2])
    yp, ys = _l1(x1, _mod_rows(mod1_raw), l1_norm1, w_in1, l1_conv_w, w_out1, l1_norm2, w1_1, w2_1, final_norm)

    return (yp.reshape(N_CTX_SEQ, CTX_LEN, D_MODEL), ys.reshape(N_LAT_SEQ, LAT_LEN, D_MODEL), new_state)
```

```python
import jax
import jax.numpy as jnp
from jax import lax
from jax.experimental import pallas as pl
from jax.experimental.pallas import tpu as pltpu

F32 = jnp.float32
BF16 = jnp.bfloat16

D_MODEL = 1024
N_CTX_SEQ = 16
CTX_LEN = 256
N_LAT_SEQ = 2
LAT_LEN = 1024
GRID_W = 64
N_CTX_TOK = N_CTX_SEQ * CTX_LEN
N_LAT_TOK = N_LAT_SEQ * LAT_LEN
N_TOK = N_CTX_TOK + N_LAT_TOK
N_HEADS = 4
DK = 128
DV = 256
CHUNK = 128
N_GROUPS = 4
GMLP_CH = 128
QK_W = N_HEADS * DK
V_W = N_HEADS * DV
GMLP_W = N_GROUPS * GMLP_CH
IN0_W = 2 * QK_W + 2 * V_W + 2 * GMLP_W
OUT0_W = V_W + GMLP_W
D_FF = 4 * D_MODEL
N_MOD = 6
MOD_W = N_MOD * D_MODEL
EPS = 1e-6
N_COND = 3
COND_ROWS = 16

PROJ_W = 2 * V_W + 2 * GMLP_W + QK_W
PROJ_V, PROJ_G, PROJ_U, PROJ_Z, PROJ_Q = 0, V_W, 2 * V_W, 2 * V_W + GMLP_W, 2 * V_W + 2 * GMLP_W

TILE = 256
N_TILES = N_TOK // TILE
N_CTX_TILES = N_CTX_TOK // TILE
N_LAT_TILES = N_LAT_TOK // TILE
LAT_TILES_PER_SEQ = LAT_LEN // TILE
BIG_TILE = 512
N_BIG_TILES = N_TOK // BIG_TILE
N_CTX_BIG = N_CTX_TOK // BIG_TILE
N_LAT_BIG = N_LAT_TOK // BIG_TILE
MOD_COL_TILE = 1536
N_FF_BLOCKS = 4
FF_BLOCK = D_FF // N_FF_BLOCKS
VMEM_LIMIT_BYTES = 56 * 1024 * 1024


def _params(n_axes):
    return pltpu.CompilerParams(dimension_semantics=("arbitrary",) * n_axes,
                                vmem_limit_bytes=VMEM_LIMIT_BYTES)


def _resident(shape):
    return pl.BlockSpec(shape, lambda *_: (0,) * len(shape), pipeline_mode=pl.Buffered(1))


def _lat_first(step, n_ctx, n_lat):
    return jnp.where(step < n_lat, step + n_ctx, step - n_lat)


def _rms(x):
    return x * lax.rsqrt(jnp.mean(x * x, axis=-1, keepdims=True) + EPS)


def _rms_mod(x, norm_w, shift, scale):
    return (_rms(x) * norm_w) * (1.0 + scale) + shift


def _dot(a, b):
    return jnp.dot(a.astype(BF16), b.astype(BF16), preferred_element_type=F32)


def _cast_specs(weights, n_slabs):
    return [pl.BlockSpec((w.shape[0] // n_slabs, w.shape[1]), lambda s: (jnp.minimum(s, n_slabs - 1), 0))
            for w in weights]


def _cast_out_shapes(weights):
    return [jax.ShapeDtypeStruct(w.shape, BF16) for w in weights]


def _cast_slabs(step, n_slabs, in_refs, out_refs):
    @pl.when(step < n_slabs)
    def _():
        for src, dst in zip(in_refs, out_refs):
            dst[...] = src[...].astype(BF16)


def _mod_kernel(cond_ref, w_ref, b_ref, cast_ref, o_ref, cast_out_ref):
    o_ref[...] = _dot(jax.nn.silu(cond_ref[...]), w_ref[...]) + b_ref[...]
    cast_out_ref[...] = cast_ref[...].astype(BF16)


def _mod_rows(raw):
    return raw[:N_COND].reshape(N_COND, N_MOD, D_MODEL)


def _mod_table(cond, w_mod, b_mod, cast_weight):
    n_steps = MOD_W // MOD_COL_TILE
    cast_spec = pl.BlockSpec((cast_weight.shape[0] // n_steps, cast_weight.shape[1]), lambda j: (j, 0))
    return pl.pallas_call(
        _mod_kernel,
        grid=(n_steps,),
        in_specs=[
            pl.BlockSpec((COND_ROWS, D_MODEL), lambda j: (0, 0)),
            pl.BlockSpec((D_MODEL, MOD_COL_TILE), lambda j: (0, j)),
            pl.BlockSpec((1, MOD_COL_TILE), lambda j: (0, j)),
            cast_spec,
        ],
        out_specs=[pl.BlockSpec((COND_ROWS, MOD_COL_TILE), lambda j: (0, j)), cast_spec],
        out_shape=[jax.ShapeDtypeStruct((COND_ROWS, MOD_W), F32), jax.ShapeDtypeStruct(cast_weight.shape, BF16)],
        compiler_params=_params(1),
        name="adaln_table",
    )(cond, w_mod, b_mod.reshape(1, MOD_W), cast_weight)


IN0_CAST_SLABS = 8


def _in0_kernel(xp_ref, xs_ref, mod_ref, nw_ref, w_ref, cond_ref, wmod_ref, bmod_ref, *rest):
    n_cast = (len(rest) - 6) // 2
    cast_in, o_ref, kt_ref, mod_out_ref = rest[:n_cast], rest[n_cast], rest[n_cast + 1], rest[n_cast + 2]
    cast_out = rest[n_cast + 3:2 * n_cast + 3]
    raw_z, raw_u, raw_g = rest[2 * n_cast + 3:]
    step = pl.program_id(0)
    _cast_slabs(step, IN0_CAST_SLABS, cast_in, cast_out)
    mod_out_ref[...] = _dot(jax.nn.silu(cond_ref[...]), wmod_ref[...]) + bmod_ref[...]

    x = jnp.where(step < N_CTX_BIG, xp_ref[...], xs_ref[...])
    h = _rms_mod(x, nw_ref[...], mod_ref[0, 0:1, :], mod_ref[0, 1:2, :]).astype(BF16)

    def proj(lo, width):
        return jnp.dot(h, w_ref[:, lo:lo + width], preferred_element_type=F32)

    raw_z[...] = proj(2 * QK_W + 2 * V_W + GMLP_W, GMLP_W)
    raw_u[...] = proj(2 * QK_W + 2 * V_W, GMLP_W)
    raw_g[...] = proj(2 * QK_W + V_W, V_W)
    o_ref[:, PROJ_Q:PROJ_Q + QK_W] = proj(0, QK_W).astype(BF16)
    z = jax.nn.gelu(raw_z[...])
    zc = z - jnp.mean(z, axis=-1, keepdims=True)
    o_ref[:, PROJ_Z:PROJ_Z + GMLP_W] = (zc * lax.rsqrt(jnp.mean(zc * zc, axis=-1, keepdims=True) + EPS)).astype(BF16)
    kt_ref[...] = (proj(QK_W, QK_W) * (DK ** -0.5)).T.astype(BF16)
    o_ref[:, PROJ_U:PROJ_U + GMLP_W] = jax.nn.gelu(raw_u[...]).astype(BF16)
    o_ref[:, PROJ_V:PROJ_V + V_W] = proj(2 * QK_W, V_W).astype(BF16)
    o_ref[:, PROJ_G:PROJ_G + V_W] = jax.nn.silu(raw_g[...]).astype(BF16)


def _big_cond(tile):
    return jnp.where(tile < N_CTX_BIG, 0, 1 + (tile - N_CTX_BIG) // (N_LAT_BIG // N_LAT_SEQ))


def _in0(xp, xs, mod, norm_w, w_in, cond, w_mod_next, b_mod_next, cast_weights):
    mod_cols = MOD_W // N_BIG_TILES
    cast_specs = _cast_specs(cast_weights, IN0_CAST_SLABS)
    return pl.pallas_call(
        _in0_kernel,
        grid=(N_BIG_TILES,),
        in_specs=[
            pl.BlockSpec((BIG_TILE, D_MODEL), lambda s: (jnp.minimum(s, N_CTX_BIG - 1), 0)),
            pl.BlockSpec((BIG_TILE, D_MODEL), lambda s: (jnp.maximum(s - N_CTX_BIG, 0), 0)),
            pl.BlockSpec((1, N_MOD, D_MODEL), lambda s: (_big_cond(s), 0, 0)),
            _resident((1, D_MODEL)),
            _resident((D_MODEL, IN0_W)),
            _resident((COND_ROWS, D_MODEL)),
            pl.BlockSpec((D_MODEL, mod_cols), lambda s: (0, s)),
            pl.BlockSpec((1, mod_cols), lambda s: (0, s)),
        ] + cast_specs,
        out_specs=[
            pl.BlockSpec((BIG_TILE, PROJ_W), lambda s: (s, 0)),
            pl.BlockSpec((QK_W, BIG_TILE), lambda s: (0, s)),
            pl.BlockSpec((COND_ROWS, mod_cols), lambda s: (0, s)),
        ] + cast_specs,
        out_shape=[jax.ShapeDtypeStruct((N_TOK, PROJ_W), BF16),
                   jax.ShapeDtypeStruct((QK_W, N_TOK), BF16),
                   jax.ShapeDtypeStruct((COND_ROWS, MOD_W), F32)] + _cast_out_shapes(cast_weights),
        scratch_shapes=[pltpu.VMEM((BIG_TILE, GMLP_W), F32),
                        pltpu.VMEM((BIG_TILE, GMLP_W), F32),
                        pltpu.VMEM((BIG_TILE, V_W), F32)],
        compiler_params=_params(1),
        name="l0_in_proj",
    )(xp, xs, mod, norm_w.reshape(1, D_MODEL), w_in, cond, w_mod_next, b_mod_next.reshape(1, MOD_W),
      *cast_weights)


def _build_decays(lg_ref, q_decay, k_decay, tile_decay):
    q_pos = lax.broadcasted_iota(jnp.int32, (TILE, 1), 0)
    k_pos = lax.broadcasted_iota(jnp.int32, (1, TILE), 1)
    for d in range(2):
        qp = (q_pos if d == 0 else TILE - 1 - q_pos).astype(F32)
        kp = (k_pos if d == 0 else TILE - 1 - k_pos).astype(F32)
        for h in range(N_HEADS):
            lg = lg_ref[d, h]
            if q_decay is not None:
                q_decay[d, h] = jnp.exp((qp + 1.0) * lg)
            k_decay[d, h] = jnp.exp((TILE - 1.0 - kp) * lg)
            tile_decay[d, h] = jnp.exp(jnp.full((1, DV), TILE, F32) * lg)


def _state_update(s, k_t, v, k_decay, tile_decay):
    k_dec = (k_t.astype(F32) * k_decay).astype(BF16)
    return tile_decay * s + jnp.dot(k_dec, v, preferred_element_type=F32)


def _lat_state_kernel(lg_ref, *refs):
    n_scan = 2 * N_LAT_SEQ
    kt_refs, v_refs = refs[:n_scan], refs[n_scan:2 * n_scan]
    s0_ref, sin_f_ref, sin_b_ref, state, k_decay, tile_decay = refs[2 * n_scan:]
    t = pl.program_id(0)

    @pl.when(t == 0)
    def _():
        _build_decays(lg_ref, None, k_decay, tile_decay)
        state[...] = s0_ref[...]

    sin_f_ref[:, 0] = state[:, 0]
    sin_b_ref[:, 0] = state[:, 1]
    for b in range(N_LAT_SEQ):
        for d in range(2):
            kt_ref, v_ref = kt_refs[2 * b + d], v_refs[2 * b + d]
            for h in range(N_HEADS):
                state[b, d, h] = _state_update(state[b, d, h], kt_ref[h * DK:(h + 1) * DK, :],
                                               v_ref[:, h * DV:(h + 1) * DV], k_decay[d, h], tile_decay[d, h])


def _lat_states(log_gamma, proj, k_t, s0):
    def tile(b, d, t):
        return N_CTX_TILES + b * LAT_TILES_PER_SEQ + (t if d == 0 else LAT_TILES_PER_SEQ - 1 - t)

    scans = [(b, d) for b in range(N_LAT_SEQ) for d in range(2)]
    state_block = (N_LAT_SEQ, 1, N_HEADS, DK, DV)
    shape = jax.ShapeDtypeStruct((N_LAT_SEQ, LAT_TILES_PER_SEQ, N_HEADS, DK, DV), F32)
    sin_f, sin_b = pl.pallas_call(
        _lat_state_kernel,
        grid=(LAT_TILES_PER_SEQ,),
        in_specs=[pl.BlockSpec(memory_space=pltpu.SMEM)]
        + [pl.BlockSpec((QK_W, TILE), lambda t, b=b, d=d: (0, tile(b, d, t))) for b, d in scans]
        + [pl.BlockSpec((TILE, V_W), lambda t, b=b, d=d: (tile(b, d, t), PROJ_V // V_W)) for b, d in scans]
        + [_resident((N_LAT_SEQ, 2, N_HEADS, DK, DV))],
        out_specs=[pl.BlockSpec(state_block, lambda t: (0, t, 0, 0, 0)),
                   pl.BlockSpec(state_block, lambda t: (0, LAT_TILES_PER_SEQ - 1 - t, 0, 0, 0))],
        out_shape=[shape, shape],
        scratch_shapes=[
            pltpu.VMEM((N_LAT_SEQ, 2, N_HEADS, DK, DV), F32),
            pltpu.VMEM((2, N_HEADS, 1, TILE), F32),
            pltpu.VMEM((2, N_HEADS, 1, DV), F32),
        ],
        compiler_params=_params(1),
        name="l0_latent_states",
    )(log_gamma, *([k_t] * len(scans)), *([proj] * len(scans)), s0)
    return (sin_f.reshape(N_LAT_TILES, N_HEADS, DK, DV), sin_b.reshape(N_LAT_TILES, N_HEADS, DK, DV))


def _mlp_residual(x, mod_ref, nw_ref, w1_ref, w2_ref):
    h = _rms_mod(x, nw_ref[...], mod_ref[0, 3:4, :], mod_ref[0, 4:5, :]).astype(BF16)
    mlp = None
    for j in range(N_FF_BLOCKS):
        cols = slice(j * FF_BLOCK, (j + 1) * FF_BLOCK)
        a = jnp.maximum(jnp.dot(h, w1_ref[:, cols], preferred_element_type=F32), 0.0)
        part = jnp.dot((a * a).astype(BF16), w2_ref[cols, :], preferred_element_type=F32)
        mlp = part if mlp is None else mlp + part
    return x + mod_ref[0, 5:6, :] * mlp


OUT0_CAST_SLABS = 16


def _out0_kernel(lg_ref, xp_ref, xs_ref, q_ref, kt_ref, v_ref, g_ref, uz_ref, sf_ref, sb_ref, mod_ref,
                 nw_ref, ws_ref, gb_ref, wout_ref, w1_ref, w2_ref, *rest):
    n_cast = (len(rest) - 6) // 2
    cast_in, y_ref, sfin_ref = rest[:n_cast], rest[n_cast], rest[n_cast + 1]
    cast_out = rest[n_cast + 2:2 * n_cast + 2]
    mask, q_decay, k_decay, tile_decay = rest[2 * n_cast + 2:]
    step = pl.program_id(0)
    is_ctx = step >= N_LAT_TILES

    @pl.when(step == 0)
    def _():
        row = lax.broadcasted_iota(jnp.int32, (TILE, TILE), 0)
        col = lax.broadcasted_iota(jnp.int32, (TILE, TILE), 1)
        diff = (row - col).astype(F32)
        for h in range(N_HEADS):
            fwd = jnp.where(diff >= 0, jnp.exp(jnp.maximum(diff, 0.0) * lg_ref[0, h]), 0.0)
            bwd = jnp.where(-diff >= 0, jnp.exp(jnp.maximum(-diff, 0.0) * lg_ref[1, h]), 0.0)
            mask[h] = fwd + bwd
        _build_decays(lg_ref, q_decay, k_decay, tile_decay)

    _cast_slabs(step, OUT0_CAST_SLABS, cast_in, cast_out)

    heads = range(N_HEADS)
    q = [q_ref[:, h * DK:(h + 1) * DK] for h in heads]
    k_t = [kt_ref[h * DK:(h + 1) * DK, :] for h in heads]
    v = [v_ref[:, h * DV:(h + 1) * DV] for h in heads]
    s_f = [jnp.where(is_ctx, 0.0, sf_ref[0, h]) for h in heads]
    s_b = [jnp.where(is_ctx, 0.0, sb_ref[0, h]) for h in heads]
    scores = [(jnp.dot(q[h], k_t[h], preferred_element_type=F32) * mask[h]).astype(BF16) for h in heads]
    cross = [jnp.dot(q[h], s_f[h].astype(BF16), preferred_element_type=F32) * q_decay[0, h]
             + jnp.dot(q[h], s_b[h].astype(BF16), preferred_element_type=F32) * q_decay[1, h] for h in heads]
    rows = []
    for n in range(TILE // CHUNK):
        groups = []
        for g in range(N_GROUPS):
            uc = uz_ref[n * CHUNK:(n + 1) * CHUNK, g * GMLP_CH:(g + 1) * GMLP_CH]
            zc = uz_ref[n * CHUNK:(n + 1) * CHUNK, GMLP_W + g * GMLP_CH:GMLP_W + (g + 1) * GMLP_CH]
            sv = jnp.dot(ws_ref[g], zc, preferred_element_type=F32) + gb_ref[:, g:g + 1]
            groups.append(uc.astype(F32) * sv)
        rows.append(jnp.concatenate(groups, axis=-1))
    mix = _dot(jnp.concatenate(rows, axis=0), wout_ref[V_W:, :])
    for h in heads:
        sfin_ref[0, 0, h] = _state_update(s_f[h], k_t[h], v[h], k_decay[0, h], tile_decay[0, h])
        sfin_ref[0, 1, h] = _state_update(s_b[h], k_t[h], v[h], k_decay[1, h], tile_decay[1, h])
    o = [jnp.dot(scores[h], v[h], preferred_element_type=F32) + cross[h] for h in heads]
    for h in heads:
        mix += _dot(g_ref[:, h * DV:(h + 1) * DV].astype(F32) * _rms(o[h]), wout_ref[h * DV:(h + 1) * DV, :])

    x1 = jnp.where(is_ctx, xp_ref[...], xs_ref[...]) + mod_ref[0, 2:3, :] * mix
    y_ref[...] = _mlp_residual(x1, mod_ref, nw_ref, w1_ref, w2_ref)


def _out0(log_gamma, xp, xs, proj, k_t, sin_f, sin_b, mod, norm_w, gmlp_ws, gmlp_b_t, w_out, w1, w2, cast_weights):
    def tile(s):
        return _lat_first(s, N_CTX_TILES, N_LAT_TILES)

    def cond(s):
        return jnp.where(s < N_LAT_TILES, 1 + s // LAT_TILES_PER_SEQ, 0)

    entry_spec = pl.BlockSpec((1, N_HEADS, DK, DV), lambda s: (jnp.minimum(s, N_LAT_TILES - 1), 0, 0, 0))
    cast_specs = _cast_specs(cast_weights, OUT0_CAST_SLABS)
    return pl.pallas_call(
        _out0_kernel,
        grid=(N_TILES,),
        in_specs=[
            pl.BlockSpec(memory_space=pltpu.SMEM),
            pl.BlockSpec((TILE, D_MODEL), lambda s: (jnp.maximum(s - N_LAT_TILES, 0), 0)),
            pl.BlockSpec((TILE, D_MODEL), lambda s: (jnp.minimum(s, N_LAT_TILES - 1), 0)),
            pl.BlockSpec((TILE, QK_W), lambda s: (tile(s), PROJ_Q // QK_W)),
            pl.BlockSpec((QK_W, TILE), lambda s: (0, tile(s))),
            pl.BlockSpec((TILE, V_W), lambda s: (tile(s), PROJ_V // V_W)),
            pl.BlockSpec((TILE, V_W), lambda s: (tile(s), PROJ_G // V_W)),
            pl.BlockSpec((TILE, 2 * GMLP_W), lambda s: (tile(s), PROJ_U // (2 * GMLP_W))),
            entry_spec,
            entry_spec,
            pl.BlockSpec((1, N_MOD, D_MODEL), lambda s: (cond(s), 0, 0)),
            _resident((1, D_MODEL)),
            _resident((N_GROUPS, CHUNK, CHUNK)),
            _resident((CHUNK, N_GROUPS)),
            _resident((OUT0_W, D_MODEL)),
            _resident((D_MODEL, D_FF)),
            _resident((D_FF, D_MODEL)),
        ] + cast_specs,
        out_specs=[
            pl.BlockSpec((TILE, D_MODEL), lambda s: (tile(s), 0)),
            pl.BlockSpec((1, 2, N_HEADS, DK, DV), lambda s: (jnp.maximum(s - N_LAT_TILES, 0), 0, 0, 0, 0)),
        ] + cast_specs,
        out_shape=[jax.ShapeDtypeStruct((N_TOK, D_MODEL), F32),
                   jax.ShapeDtypeStruct((N_CTX_SEQ, 2, N_HEADS, DK, DV), F32)] + _cast_out_shapes(cast_weights),
        scratch_shapes=[
            pltpu.VMEM((N_HEADS, TILE, TILE), F32),
            pltpu.VMEM((2, N_HEADS, TILE, 1), F32),
            pltpu.VMEM((2, N_HEADS, 1, TILE), F32),
            pltpu.VMEM((2, N_HEADS, 1, DV), F32),
        ],
        compiler_params=_params(1),
        name="l0_mix_out_mlp",
    )(log_gamma, xp, xs, proj, k_t, proj, proj, proj, sin_f, sin_b, mod, norm_w.reshape(1, D_MODEL), gmlp_ws,
      gmlp_b_t, w_out, w1, w2, *cast_weights)


def _l1_kernel(x_ref, mod_ref, n1_ref, win_ref, cw_ref, wout_ref, n2_ref, w1_ref, w2_ref, fn_ref,
               yp_ref, ys_ref):
    step = pl.program_id(0)
    is_ctx = step >= N_LAT_BIG
    x = x_ref[...]
    h = _rms_mod(x, n1_ref[...], mod_ref[0, 0:1, :], mod_ref[0, 1:2, :]).astype(BF16)
    cg = jnp.dot(h, win_ref[:, D_MODEL:2 * D_MODEL], preferred_element_type=F32)
    hv = jnp.dot(h, win_ref[:, 2 * D_MODEL:3 * D_MODEL], preferred_element_type=F32)
    xc = cg * hv
    period = jnp.where(is_ctx, CTX_LEN, GRID_W)
    t = lax.broadcasted_iota(jnp.int32, (BIG_TILE, 1), 0) & (period - 1)
    prev = jnp.where(t == 0, 0.0, pltpu.roll(xc, 1, 0))
    nxt = jnp.where(t == period - 1, 0.0, pltpu.roll(xc, BIG_TILE - 1, 0))
    yc = cw_ref[0:1, :] * prev + cw_ref[1:2, :] * xc + cw_ref[2:3, :] * nxt
    bg = jnp.dot(h, win_ref[:, 0:D_MODEL], preferred_element_type=F32)
    x1 = x + mod_ref[0, 2:3, :] * _dot(bg * yc, wout_ref[...])
    y = _rms(_mlp_residual(x1, mod_ref, n2_ref, w1_ref, w2_ref)) * fn_ref[...]
    yp_ref[...] = y
    ys_ref[...] = jnp.where(is_ctx, ys_ref[...], y)


def _l1(x, mod, norm1, w_in, conv_w, w_out, norm2, w1, w2, final_norm):
    return pl.pallas_call(
        _l1_kernel,
        grid=(N_BIG_TILES,),
        in_specs=[
            pl.BlockSpec((BIG_TILE, D_MODEL), lambda s: (_lat_first(s, N_CTX_BIG, N_LAT_BIG), 0)),
            pl.BlockSpec((1, N_MOD, D_MODEL), lambda s: (_big_cond(_lat_first(s, N_CTX_BIG, N_LAT_BIG)), 0, 0)),
            _resident((1, D_MODEL)),
            _resident((D_MODEL, 3 * D_MODEL)),
            _resident((3, D_MODEL)),
            _resident((D_MODEL, D_MODEL)),
            _resident((1, D_MODEL)),
            _resident((D_MODEL, D_FF)),
            _resident((D_FF, D_MODEL)),
            _resident((1, D_MODEL)),
        ],
        out_specs=[
            pl.BlockSpec((BIG_TILE, D_MODEL), lambda s: (jnp.maximum(s - N_LAT_BIG, 0), 0)),
            pl.BlockSpec((BIG_TILE, D_MODEL), lambda s: (jnp.minimum(s, N_LAT_BIG - 1), 0)),
        ],
        out_shape=[jax.ShapeDtypeStruct((N_CTX_TOK, D_MODEL), F32),
                   jax.ShapeDtypeStruct((N_LAT_TOK, D_MODEL), F32)],
        compiler_params=_params(1),
        name="l1_conv_mlp_norm",
    )(x, mod, norm1.reshape(1, D_MODEL), w_in, conv_w, w_out, norm2.reshape(1, D_MODEL), w1, w2,
      final_norm.reshape(1, D_MODEL))


def kernel(x_prompt, x_sample, state_l0_ret, c, c_ctx, l0_norm1, l0_w_in, l0_ret_decay_exp, l0_gmlp_ws,
           l0_gmlp_b, l0_w_out, l0_norm2, l0_w_mod, l0_b_mod, l0_ffn_w1, l0_ffn_w2, l1_norm1, l1_w_in,
           l1_conv_w, l1_w_out, l1_norm2, l1_w_mod, l1_b_mod, l1_ffn_w1, l1_ffn_w2, final_norm):
    xp = x_prompt.reshape(N_CTX_TOK, D_MODEL)
    xs = x_sample.reshape(N_LAT_TOK, D_MODEL)
    cond = jnp.concatenate([c_ctx[None, :], c, jnp.zeros((COND_ROWS - N_COND, D_MODEL), F32)], axis=0)
    log_gamma = jnp.log1p(-jnp.exp2(-l0_ret_decay_exp.astype(F32)))

    mod0_raw, w_in0 = _mod_table(cond, l0_w_mod, l0_b_mod, l0_w_in)
    mod0 = _mod_rows(mod0_raw)
    proj, k_t, mod1_raw, w_out0, w1_0, w2_0 = _in0(xp, xs, mod0, l0_norm1, w_in0, cond, l1_w_mod, l1_b_mod,
                                                   [l0_w_out, l0_ffn_w1, l0_ffn_w2])
    sin_f, sin_b = _lat_states(log_gamma, proj, k_t, state_l0_ret)
    x1, new_state, w_in1, w_out1, w1_1, w2_1 = _out0(
        log_gamma, xp, xs, proj, k_t, sin_f, sin_b, mod0, l0_norm2, l0_gmlp_ws.astype(BF16), l0_gmlp_b.T,
        w_out0, w1_0, w2_0, [l1_w_in, l1_w_out, l1_ffn_w1, l1_ffn_w2])
    yp, ys = _l1(x1, _mod_rows(mod1_raw), l1_norm1, w_in1, l1_conv_w, w_out1, l1_norm2, w1_1, w2_1, final_norm)

    return (yp.reshape(N_CTX_SEQ, CTX_LEN, D_MODEL), ys.reshape(N_LAT_SEQ, LAT_LEN, D_MODEL), new_state)
```
